```python
import jax, jax.numpy as jnp
from jax import lax
import numpy as np

D_MODEL = 1024
BATCH = 32
SEQ = 2048
DEPTH = 1

POOL_WIDTH = D_MODEL
POOL_WINDOWS = (2, 4, 8, 16)
POOL_GROUPS = len(POOL_WINDOWS)
POOL_GROUP_WIDTH = POOL_WIDTH // POOL_GROUPS
SSD_EXPAND = 2
SSD_INNER = SSD_EXPAND * D_MODEL
SSD_HEAD_DIM = 64
SSD_HEADS = SSD_INNER // SSD_HEAD_DIM
SSD_GROUPS = 8
SSD_HEADS_PER_GROUP = SSD_HEADS // SSD_GROUPS
SSD_STATE = 128
SSD_CONV = 4
SSD_CHUNK = 256
SSD_CONV_DIM = SSD_INNER + 2 * SSD_GROUPS * SSD_STATE
SSD_NORM_GROUP = SSD_INNER // SSD_GROUPS
D_FF = 4 * D_MODEL
N_BRANCHES = 2
OFF_POOL = POOL_WIDTH
OFF_Z = OFF_POOL + SSD_INNER
OFF_XBC = OFF_Z + SSD_CONV_DIM
OFF_DT = OFF_XBC + SSD_HEADS
IN_PROJ_WIDTH = OFF_DT + N_BRANCHES * D_MODEL
DEEPNORM_ALPHA = (2.0 * DEPTH) ** 0.25
DEEPNORM_BETA = (8.0 * DEPTH) ** -0.25
LN_EPS = 1e-5
RMS_EPS = 1e-5

kernel_name = "pool_ssd_gated_hybrid_deepnorm"


def layer_norm(x, g, b):
    xf = x.astype(jnp.float32)
    mu = jnp.mean(xf, axis=-1, keepdims=True)
    var = jnp.mean(jnp.square(xf - mu), axis=-1, keepdims=True)
    return ((xf - mu) * lax.rsqrt(var + LN_EPS) * g + b).astype(x.dtype)


def causal_multiscale_pool(u):
    bsz, s, _ = u.shape
    uf = u.astype(jnp.float32).reshape(bsz, s, POOL_GROUPS, POOL_GROUP_WIDTH)
    cs = jnp.cumsum(uf, axis=1)
    pos = jnp.arange(1, s + 1, dtype=jnp.float32)
    outs = []
    for gi, w in enumerate(POOL_WINDOWS):
        csg = cs[:, :, gi]
        lag = jnp.pad(csg, ((0, 0), (w, 0), (0, 0)))[:, :s]
        cnt = jnp.minimum(pos, float(w))[None, :, None]
        outs.append((csg - lag) / cnt)
    pooled = jnp.stack(outs, axis=2)
    return pooled - uf


def causal_depthwise_conv(u, w, b):
    k_width = w.shape[0]
    s = u.shape[1]
    up = jnp.pad(u, ((0, 0), (k_width - 1, 0), (0, 0)))
    y = up[:, 0:s] * w[0]
    for k in range(1, k_width):
        y = y + up[:, k:k + s] * w[k]
    return y + b


def segsum_exp(a):
    t = a.shape[-1]
    cs = jnp.cumsum(a, axis=-1)
    seg = cs[..., :, None] - cs[..., None, :]
    mask = jnp.tril(jnp.ones((t, t), dtype=bool))
    return jnp.exp(jnp.where(mask, seg, -jnp.inf))


def ssd_chunked(xdt, da, bm, cm):
    bsz, s = xdt.shape[:2]
    n_chunks = -(-s // SSD_CHUNK)
    pad = n_chunks * SSD_CHUNK - s

    def to_chunks(t):
        t = jnp.pad(t, ((0, 0), (0, pad)) + ((0, 0),) * (t.ndim - 2))
        return t.reshape((bsz, n_chunks, SSD_CHUNK) + t.shape[2:])

    xc, ac, bc, cc = to_chunks(xdt), to_chunks(da), to_chunks(bm), to_chunks(cm)
    a_cs = jnp.cumsum(ac, axis=2)
    lmat = segsum_exp(jnp.moveaxis(ac, 2, -1))
    cb = jnp.einsum('bclgn,bcsgn->bcgls', cc, bc)
    y_diag = jnp.einsum('bcgls,bcgrls,bcsgrp->bclgrp', cb, lmat, xc)
    decay_to_end = jnp.exp(a_cs[:, :, -1:] - a_cs)
    states = jnp.einsum('bclgn,bclgr,bclgrp->bcgrpn', bc, decay_to_end, xc)
    chunk_decay = jnp.exp(a_cs[:, :, -1])

    def step(h, inp):
        s_c, d_c = inp
        return d_c[..., None, None] * h + s_c, h

    h0 = jnp.zeros_like(states[:, 0])
    _, prev = lax.scan(step, h0, (jnp.moveaxis(states, 1, 0), jnp.moveaxis(chunk_decay, 1, 0)))
    prev = jnp.moveaxis(prev, 0, 1)
    y_off = jnp.einsum('bclgn,bcgrpn,bclgr->bclgrp', cc, prev, jnp.exp(a_cs))
    y = (y_diag + y_off).reshape((bsz, n_chunks * SSD_CHUNK) + xdt.shape[2:])
    return y[:, :s]


def hybrid_layer(h, w_in, b_gates, conv_w, conv_b, dt_bias, a_log, d_skip,
                 ssd_norm_w, w_ssd_proj, w_pool_group, pool_scale, w_out,
                 ln1_g, ln1_b, w_up, w_down, ln2_g, ln2_b):
    bsz, s, _ = h.shape
    proj = h @ w_in
    u_pool, z, xbc, dt_raw, gate_logits = jnp.split(
        proj, [OFF_POOL, OFF_Z, OFF_XBC, OFF_DT], axis=-1)
    gates = jax.nn.sigmoid((gate_logits + b_gates).astype(jnp.float32))
    gates = gates.reshape(bsz, s, N_BRANCHES, D_MODEL)

    pooled = causal_multiscale_pool(u_pool)
    y_pool = jnp.einsum('bsgc,gcd->bsgd', pooled, w_pool_group)
    y_pool = y_pool.reshape(bsz, s, POOL_WIDTH) * pool_scale

    xbc = jax.nn.silu(causal_depthwise_conv(xbc, conv_w, conv_b))
    xs, bm, cm = jnp.split(xbc, [SSD_INNER, SSD_INNER + SSD_GROUPS * SSD_STATE], axis=-1)
    xs = xs.astype(jnp.float32).reshape(bsz, s, SSD_GROUPS, SSD_HEADS_PER_GROUP, SSD_HEAD_DIM)
    bm = bm.astype(jnp.float32).reshape(bsz, s, SSD_GROUPS, SSD_STATE)
    cm = cm.astype(jnp.float32).reshape(bsz, s, SSD_GROUPS, SSD_STATE)
    dt = jax.nn.softplus(dt_raw.astype(jnp.float32) + dt_bias)
    dt = dt.reshape(bsz, s, SSD_GROUPS, SSD_HEADS_PER_GROUP)
    a = -jnp.exp(a_log.astype(jnp.float32)).reshape(SSD_GROUPS, SSD_HEADS_PER_GROUP)
    d = d_skip.astype(jnp.float32).reshape(SSD_GROUPS, SSD_HEADS_PER_GROUP)
    y = ssd_chunked(xs * dt[..., None], dt * a, bm, cm) + d[..., None] * xs
    y = y.reshape(bsz, s, SSD_INNER)
    yg = (y * jax.nn.silu(z.astype(jnp.float32))).reshape(bsz, s, SSD_GROUPS, SSD_NORM_GROUP)
    yg = yg * lax.rsqrt(jnp.mean(jnp.square(yg), axis=-1, keepdims=True) + RMS_EPS)
    yg = yg.reshape(bsz, s, SSD_INNER) * ssd_norm_w
    y_ssd = yg.astype(h.dtype) @ w_ssd_proj

    merged = gates[:, :, 0] * y_pool + gates[:, :, 1] * y_ssd
    mix = merged.astype(h.dtype) @ w_out
    h = layer_norm(DEEPNORM_ALPHA * h + mix, ln1_g, ln1_b)

    ff = jnp.square(jax.nn.relu(h @ w_up)) @ w_down
    h = layer_norm(DEEPNORM_ALPHA * h + ff, ln2_g, ln2_b)
    return h


def setup_inputs(seed: int = 0) -> dict:
    key = jax.random.key(seed)
    ks = jax.random.split(key, 20)
    nrm = lambda k, shape: jax.random.normal(k, shape, dtype=jnp.float32)
    x = nrm(ks[0], (BATCH, SEQ, D_MODEL))
    w_in = nrm(ks[1], (DEPTH, D_MODEL, IN_PROJ_WIDTH)) * D_MODEL ** -0.5
    b_gates = 0.1 * nrm(ks[2], (DEPTH, N_BRANCHES * D_MODEL))
    conv_w = nrm(ks[3], (DEPTH, SSD_CONV, SSD_CONV_DIM)) * SSD_CONV ** -0.5
    conv_b = 0.02 * nrm(ks[4], (DEPTH, SSD_CONV_DIM))
    dt0 = jnp.exp(jax.random.uniform(ks[5], (DEPTH, SSD_HEADS), dtype=jnp.float32,
                                     minval=np.log(1e-3), maxval=np.log(1e-1)))
    dt_bias = dt0 + jnp.log(-jnp.expm1(-dt0))
    a_log = jnp.log(jax.random.uniform(ks[6], (DEPTH, SSD_HEADS), dtype=jnp.float32,
                                       minval=1.0, maxval=16.0))
    d_skip = 1.0 + 0.1 * nrm(ks[7], (DEPTH, SSD_HEADS))
    ssd_norm_w = 1.0 + 0.02 * nrm(ks[8], (DEPTH, SSD_INNER))
    w_ssd_proj = nrm(ks[9], (DEPTH, SSD_INNER, D_MODEL)) * SSD_INNER ** -0.5
    w_pool_group = nrm(ks[10], (DEPTH, POOL_GROUPS, POOL_GROUP_WIDTH, POOL_GROUP_WIDTH)) * POOL_GROUP_WIDTH ** -0.5
    pool_scale = 1.0 + 0.02 * nrm(ks[11], (DEPTH, POOL_WIDTH))
    w_out = nrm(ks[12], (DEPTH, D_MODEL, D_MODEL)) * (D_MODEL ** -0.5 * DEEPNORM_BETA)
    ln1_g = 1.0 + 0.02 * nrm(ks[13], (DEPTH, D_MODEL))
    ln1_b = 0.02 * nrm(ks[14], (DEPTH, D_MODEL))
    w_up = nrm(ks[15], (DEPTH, D_MODEL, D_FF)) * D_MODEL ** -0.5
    w_down = nrm(ks[16], (DEPTH, D_FF, D_MODEL)) * (D_FF ** -0.5 * DEEPNORM_BETA)
    ln2_g = 1.0 + 0.02 * nrm(ks[17], (DEPTH, D_MODEL))
    ln2_b = 0.02 * nrm(ks[18], (DEPTH, D_MODEL))
    return {"x": x, "w_in": w_in, "b_gates": b_gates, "conv_w": conv_w, "conv_b": conv_b,
            "dt_bias": dt_bias, "a_log": a_log, "d_skip": d_skip, "ssd_norm_w": ssd_norm_w,
            "w_ssd_proj": w_ssd_proj, "w_pool_group": w_pool_group, "pool_scale": pool_scale,
            "w_out": w_out, "ln1_g": ln1_g, "ln1_b": ln1_b, "w_up": w_up, "w_down": w_down,
            "ln2_g": ln2_g, "ln2_b": ln2_b}


def reference(x, w_in, b_gates, conv_w, conv_b, dt_bias, a_log, d_skip, ssd_norm_w,
              w_ssd_proj, w_pool_group, pool_scale, w_out, ln1_g, ln1_b, w_up, w_down,
              ln2_g, ln2_b):
    h = x
    for layer in range(DEPTH):
        h = hybrid_layer(h, w_in[layer], b_gates[layer], conv_w[layer], conv_b[layer],
                         dt_bias[layer], a_log[layer], d_skip[layer], ssd_norm_w[layer],
                         w_ssd_proj[layer], w_pool_group[layer], pool_scale[layer],
                         w_out[layer], ln1_g[layer], ln1_b[layer], w_up[layer],
                         w_down[layer], ln2_g[layer], ln2_b[layer])
    return h
```

```python
import functools

import jax
import jax.numpy as jnp
from jax import lax
from jax.experimental import pallas as pl
from jax.experimental.pallas import tpu as pltpu

F32 = jnp.float32
BF16 = jnp.bfloat16

D_MODEL = 1024
POOL_WINDOWS = (2, 4, 8, 16)
POOL_GROUP_WIDTH = D_MODEL // len(POOL_WINDOWS)
POOL_HALO = 16
SSD_INNER = 2 * D_MODEL
SSD_HEAD_DIM = 64
SSD_HEADS = SSD_INNER // SSD_HEAD_DIM
SSD_GROUPS = 8
SSD_HEADS_PER_GROUP = SSD_HEADS // SSD_GROUPS
SSD_GROUP_WIDTH = SSD_INNER // SSD_GROUPS
SSD_STATE = 128
SSD_CONV = 4
CONV_HALO = 8
SSD_CHUNK = 256
SSD_BC_WIDTH = SSD_GROUPS * SSD_STATE
SSD_CONV_DIM = SSD_INNER + 2 * SSD_BC_WIDTH
D_FF = 4 * D_MODEL
OFF_POOL = D_MODEL
OFF_Z = OFF_POOL + SSD_INNER
OFF_XBC = OFF_Z + SSD_CONV_DIM
OFF_DT = OFF_XBC + SSD_HEADS
DEEPNORM_ALPHA = 2.0 ** 0.25
LN_EPS = 1e-5
RMS_EPS = 1e-5

LANES = 128
VMEM_LIMIT_BYTES = 56 * 1024 * 1024
PROJ_TILE = 512
CONV_COLS = 1024
MLP_TILE = 1024
MLP_COLS = 1024


def _dot(a, b):
    return jnp.dot(a, b, preferred_element_type=F32)


def _const_spec(shape):
    return pl.BlockSpec(shape, lambda *_: (0,) * len(shape), pipeline_mode=pl.Buffered(1))


def _layer_norm(v, g, b):
    mu = jnp.mean(v, axis=-1, keepdims=True)
    d = v - mu
    var = jnp.mean(d * d, axis=-1, keepdims=True)
    return d * lax.rsqrt(var + LN_EPS) * g + b


def _split2(v):
    hi = v.astype(BF16)
    lo = (v - hi.astype(F32)).astype(BF16)
    return hi, lo


def _proj_kernel(x_ref, wp_ref, wz_ref, wxbc_ref, wdt_ref, wg_ref, bg_ref,
                 cw_ref, cb_ref, dtb_ref, wpg_ref, ps_ref,
                 xs_ref, bm_ref, cm_ref, dt_ref, zs_ref, gyp_ref, gb_ref,
                 ubuf, cbuf, ccarry):
    tile = x_ref.shape[1]
    step = pl.program_id(1)

    @pl.when(step == 0)
    def _():
        ubuf[0:POOL_HALO, :] = jnp.zeros((POOL_HALO, D_MODEL), F32)
        ccarry[...] = jnp.zeros(ccarry.shape, F32)

    xb = x_ref[0].astype(BF16)

    u = _dot(xb, wp_ref[...])
    ubuf[POOL_HALO:POOL_HALO + tile, :] = u
    pos = (step * tile + 1 + lax.broadcasted_iota(jnp.int32, (tile, 1), 0)).astype(F32)
    for gi, win in enumerate(POOL_WINDOWS):
        cols = slice(gi * POOL_GROUP_WIDTH, (gi + 1) * POOL_GROUP_WIDTH)
        ug = u[:, cols]
        acc = ug
        for k in range(1, win):
            acc = acc + ubuf[POOL_HALO - k:POOL_HALO - k + tile, cols]
        inv_cnt = 1.0 / jnp.minimum(pos, float(win))
        pooled = (acc * inv_cnt - ug).astype(BF16)
        y_pool = _dot(pooled, wpg_ref[gi]) * ps_ref[:, cols]
        gate = jax.nn.sigmoid(_dot(xb, wg_ref[:, cols]) + bg_ref[:, cols])
        gyp_ref[0, :, cols] = (gate * y_pool).astype(BF16)
    ubuf[0:POOL_HALO, :] = ubuf[tile:tile + POOL_HALO, :]

    gate_b = jax.nn.sigmoid(_dot(xb, wg_ref[:, D_MODEL:]) + bg_ref[:, D_MODEL:])
    gb_ref[0] = gate_b.astype(BF16)

    z = _dot(xb, wz_ref[...])
    zs_ref[0] = (z * jax.nn.sigmoid(z)).astype(BF16)

    dt_ref[0] = jax.nn.softplus(_dot(xb, wdt_ref[...]) + dtb_ref[...])

    for c in range(SSD_CONV_DIM // CONV_COLS):
        cols = slice(c * CONV_COLS, (c + 1) * CONV_COLS)
        raw = _dot(xb, wxbc_ref[:, cols])
        cbuf[0:CONV_HALO, :] = ccarry[:, cols]
        cbuf[CONV_HALO:CONV_HALO + tile, :] = raw
        acc = raw * cw_ref[SSD_CONV - 1:SSD_CONV, cols] + cb_ref[:, cols]
        for k in range(SSD_CONV - 1):
            lag = SSD_CONV - 1 - k
            acc = acc + cbuf[CONV_HALO - lag:CONV_HALO - lag + tile, :] * cw_ref[k:k + 1, cols]
        ccarry[:, cols] = cbuf[tile:tile + CONV_HALO, :]
        act = (acc * jax.nn.sigmoid(acc)).astype(BF16)
        if c * CONV_COLS < SSD_INNER:
            xs_ref[0, :, cols] = act
        elif c * CONV_COLS < SSD_INNER + SSD_BC_WIDTH:
            bm_ref[0] = act
        else:
            cm_ref[0] = act


def _proj_call(x, wp, wz, wxbc, wdt, wg, bg, cw, cb, dtb, wpg, ps):
    bsz, seq, _ = x.shape
    tile = min(PROJ_TILE, seq)
    assert seq % tile == 0 and tile % CONV_HALO == 0 and tile >= POOL_HALO
    assert CONV_COLS == SSD_BC_WIDTH and SSD_INNER % CONV_COLS == 0
    tok = lambda w: pl.BlockSpec((1, tile, w), lambda b, s: (b, s, 0))
    out_widths = (SSD_INNER, SSD_BC_WIDTH, SSD_BC_WIDTH, LANES, SSD_INNER, D_MODEL, D_MODEL)
    out_dtypes = (BF16, BF16, BF16, F32, BF16, BF16, BF16)
    return pl.pallas_call(
        _proj_kernel,
        grid=(bsz, seq // tile),
        in_specs=[tok(D_MODEL)] + [_const_spec(a.shape) for a in
                                   (wp, wz, wxbc, wdt, wg, bg, cw, cb, dtb, wpg, ps)],
        out_specs=[tok(w) for w in out_widths],
        out_shape=[jax.ShapeDtypeStruct((bsz, seq, w), d) for w, d in zip(out_widths, out_dtypes)],
        scratch_shapes=[pltpu.VMEM((tile + POOL_HALO, D_MODEL), F32),
                        pltpu.VMEM((tile + CONV_HALO, CONV_COLS), F32),
                        pltpu.VMEM((CONV_HALO, SSD_CONV_DIM), F32)],
        compiler_params=pltpu.CompilerParams(
            dimension_semantics=("arbitrary", "arbitrary"),
            vmem_limit_bytes=VMEM_LIMIT_BYTES),
        name="proj_pool_conv",
    )(x, wp, wz, wxbc, wdt, wg, bg, cw, cb, dtb, wpg, ps)


def _ssd_kernel(xs_ref, bm_ref, cm_ref, dt_ref, zs_ref, gyp_ref, gb_ref, x_ref,
                a_ref, dexp_ref, nw_ref, tri_ref, e2_ref, wsp_ref, wout_ref, g1_ref, b1_ref,
                h1_ref, st_ref, yn_ref):
    chunk = xs_ref.shape[1]

    @pl.when(pl.program_id(1) == 0)
    def _():
        st_ref[...] = jnp.zeros(st_ref.shape, F32)

    dt = dt_ref[0]
    da = dt * a_ref[...]
    hi = da.astype(BF16)
    r1 = da - hi.astype(F32)
    mid = r1.astype(BF16)
    lo = (r1 - mid.astype(F32)).astype(BF16)
    cs3 = _dot(tri_ref[...], jnp.concatenate([hi, mid, lo], axis=1))
    cs = cs3[:, 0:LANES] + cs3[:, LANES:2 * LANES] + cs3[:, 2 * LANES:3 * LANES]
    last = cs[chunk - 1:chunk, :]
    e_cs = jnp.exp(cs)
    w_end = dt * jnp.exp(last - cs)
    cdec = jnp.broadcast_to(jnp.exp(last), (8, LANES))
    cs_t = cs.T
    dt_t = dt.T

    stacked = jnp.concatenate([w_end, e_cs, cdec], axis=0)
    s_hi, s_lo = _split2(stacked)
    expanded = _dot(jnp.concatenate([s_hi, s_lo], axis=1), e2_ref[...])
    w_exp = expanded[0:chunk]
    e_exp = expanded[chunk:2 * chunk]
    cdec_exp = expanded[2 * chunk:2 * chunk + 1]

    row = lax.broadcasted_iota(jnp.int32, (chunk, chunk), 0)
    col = lax.broadcasted_iota(jnp.int32, (chunk, chunk), 1)
    causal = row >= col
    first_head = lax.broadcasted_iota(jnp.int32, (chunk, LANES), 1) < SSD_HEAD_DIM

    for g in range(SSD_GROUPS):
        scols = slice(g * SSD_STATE, (g + 1) * SSD_STATE)
        gcols = slice(g * SSD_GROUP_WIDTH, (g + 1) * SSD_GROUP_WIDTH)
        b_g = bm_ref[0, :, scols]
        c_g = cm_ref[0, :, scols]
        x_g = xs_ref[0, :, gcols]
        cb = lax.dot_general(c_g, b_g, (((1,), (1,)), ((), ())), preferred_element_type=F32)
        ys = []
        for r in range(SSD_HEADS_PER_GROUP):
            h = g * SSD_HEADS_PER_GROUP + r
            seg = cs[:, h:h + 1] - cs_t[h:h + 1, :]
            decay = jnp.where(causal, jnp.exp(seg), 0.0) * dt_t[h:h + 1, :]
            m_h = (cb * decay).astype(BF16)
            pair = (r // 2) * LANES
            ys.append(_dot(m_h, x_g[:, pair:pair + LANES]))
        y_diag = jnp.concatenate([jnp.where(first_head, ys[0], ys[1]),
                                  jnp.where(first_head, ys[2], ys[3])], axis=1)
        state_t = st_ref[g]
        y_off = _dot(c_g, state_t.astype(BF16)) * e_exp[:, gcols]
        x_f = x_g.astype(F32)
        y = y_diag + y_off + dexp_ref[:, gcols] * x_f
        x_dec = (x_f * w_exp[:, gcols]).astype(BF16)
        new = lax.dot_general(b_g, x_dec, (((0,), (0,)), ((), ())), preferred_element_type=F32)
        st_ref[g] = cdec_exp[:, gcols] * state_t + new
        yg = y * zs_ref[0, :, gcols].astype(F32)
        ms = jnp.mean(yg * yg, axis=-1, keepdims=True)
        yn_ref[:, gcols] = (yg * lax.rsqrt(ms + RMS_EPS) * nw_ref[:, gcols]).astype(BF16)

    y_ssd = _dot(yn_ref[...], wsp_ref[...])
    merged = gyp_ref[0].astype(F32) + gb_ref[0].astype(F32) * y_ssd
    mix = _dot(merged.astype(BF16), wout_ref[...])
    h1_ref[0] = _layer_norm(DEEPNORM_ALPHA * x_ref[0] + mix, g1_ref[...], b1_ref[...])


def _ssd_call(xs, bm, cm, dt, zs, gyp, gb, x, a_row, dexp, nw, tri, e2, wsp, wout, g1, b1):
    bsz, seq, _ = x.shape
    chunk = SSD_CHUNK
    assert seq % chunk == 0
    tok = lambda w: pl.BlockSpec((1, chunk, w), lambda b, c: (b, c, 0))
    consts = (a_row, dexp, nw, tri, e2, wsp, wout, g1, b1)
    return pl.pallas_call(
        _ssd_kernel,
        grid=(bsz, seq // chunk),
        in_specs=[tok(a.shape[-1]) for a in (xs, bm, cm, dt, zs, gyp, gb, x)]
                 + [_const_spec(a.shape) for a in consts],
        out_specs=tok(D_MODEL),
        out_shape=jax.ShapeDtypeStruct((bsz, seq, D_MODEL), F32),
        scratch_shapes=[pltpu.VMEM((SSD_GROUPS, SSD_STATE, SSD_GROUP_WIDTH), F32),
                        pltpu.VMEM((chunk, SSD_INNER), BF16)],
        compiler_params=pltpu.CompilerParams(
            dimension_semantics=("arbitrary", "arbitrary"),
            vmem_limit_bytes=VMEM_LIMIT_BYTES),
        name="ssd_merge_ln1",
    )(xs, bm, cm, dt, zs, gyp, gb, x, *consts)


def _mlp_kernel(h_ref, wup_ref, wdn_ref, g_ref, b_ref, o_ref):
    h = h_ref[...]
    hb = h.astype(BF16)
    ff = jnp.zeros(h.shape, F32)
    for j in range(D_FF // MLP_COLS):
        cols = slice(j * MLP_COLS, (j + 1) * MLP_COLS)
        up = jnp.maximum(_dot(hb, wup_ref[:, cols]), 0.0)
        ff = ff + _dot((up * up).astype(BF16), wdn_ref[cols, :])
    o_ref[...] = _layer_norm(DEEPNORM_ALPHA * h + ff, g_ref[...], b_ref[...])


def _mlp_call(h, wup, wdn, g, b):
    rows = h.shape[0]
    tile = min(MLP_TILE, rows)
    assert rows % tile == 0
    tok = pl.BlockSpec((tile, D_MODEL), lambda i: (i, 0))
    return pl.pallas_call(
        _mlp_kernel,
        grid=(rows // tile,),
        in_specs=[tok] + [_const_spec(a.shape) for a in (wup, wdn, g, b)],
        out_specs=tok,
        out_shape=jax.ShapeDtypeStruct(h.shape, F32),
        compiler_params=pltpu.CompilerParams(
            dimension_semantics=("arbitrary",),
            vmem_limit_bytes=VMEM_LIMIT_BYTES),
        name="mlp_ln2",
    )(h, wup, wdn, g, b)


def _layer(x, w_in, b_gates, conv_w, conv_b, dt_bias, a_log, d_skip, ssd_norm_w, w_ssd_proj,
           w_pool_group, pool_scale, w_out, ln1_g, ln1_b, w_up, w_down, ln2_g, ln2_b):
    bsz, seq, _ = x.shape
    row = lambda v: v.astype(F32).reshape(1, -1)
    pad_heads = lambda v: jnp.pad(v.astype(F32), (0, LANES - SSD_HEADS)).reshape(1, LANES)

    wb = w_in.astype(BF16)
    wp, wz, wxbc = wb[:, :OFF_POOL], wb[:, OFF_POOL:OFF_Z], wb[:, OFF_Z:OFF_XBC]
    wdt = jnp.pad(wb[:, OFF_XBC:OFF_DT], ((0, 0), (0, LANES - SSD_HEADS)))
    wg = wb[:, OFF_DT:]
    xs, bm, cm, dt, zs, gyp, gb = _proj_call(
        x, wp, wz, wxbc, wdt, wg, row(b_gates), conv_w.astype(F32), row(conv_b),
        pad_heads(dt_bias), w_pool_group.astype(BF16), row(pool_scale))

    a_row = pad_heads(-jnp.exp(a_log.astype(F32)))
    dexp = jnp.repeat(d_skip.astype(F32), SSD_HEAD_DIM).reshape(1, SSD_INNER)
    idx = jnp.arange(SSD_CHUNK)
    tri = (idx[:, None] >= idx[None, :]).astype(BF16)
    head_of = jnp.arange(SSD_INNER) // SSD_HEAD_DIM
    e1 = (jnp.arange(LANES)[:, None] == head_of[None, :]).astype(BF16)
    e2 = jnp.concatenate([e1, e1], axis=0)
    h1 = _ssd_call(xs, bm, cm, dt, zs, gyp, gb, x, a_row, dexp, row(ssd_norm_w), tri, e2,
                   w_ssd_proj.astype(BF16), w_out.astype(BF16), row(ln1_g), row(ln1_b))

    h2 = _mlp_call(h1.reshape(bsz * seq, D_MODEL), w_up.astype(BF16), w_down.astype(BF16),
                   row(ln2_g), row(ln2_b))
    return h2.reshape(bsz, seq, D_MODEL)


def kernel(x, w_in, b_gates, conv_w, conv_b, dt_bias, a_log, d_skip, ssd_norm_w, w_ssd_proj,
           w_pool_group, pool_scale, w_out, ln1_g, ln1_b, w_up, w_down, ln2_g, ln2_b):
    h = x
    for layer in range(w_in.shape[0]):
        h = _layer(h, w_in[layer], b_gates[layer], conv_w[layer], conv_b[layer], dt_bias[layer],
                   a_log[layer], d_skip[layer], ssd_norm_w[layer], w_ssd_proj[layer],
                   w_pool_group[layer], pool_scale[layer], w_out[layer], ln1_g[layer],
                   ln1_b[layer], w_up[layer], w_down[layer], ln2_g[layer], ln2_b[layer])
    return h
```

```python
import math

import jax
import jax.numpy as jnp
from jax import lax
from jax.experimental import pallas as pl
from jax.experimental.pallas import tpu as pltpu

F32 = jnp.float32
BF16 = jnp.bfloat16

D_MODEL = 1024
POOL_WINDOWS = (2, 4, 8, 16)
POOL_GROUP_WIDTH = D_MODEL // len(POOL_WINDOWS)
SSD_INNER = 2 * D_MODEL
SSD_HEAD_DIM = 64
SSD_HEADS = SSD_INNER // SSD_HEAD_DIM
SSD_GROUPS = 8
SSD_HEADS_PER_GROUP = SSD_HEADS // SSD_GROUPS
SSD_GROUP_WIDTH = SSD_INNER // SSD_GROUPS
SSD_STATE = 128
SSD_CONV = 4
SSD_BC_WIDTH = SSD_GROUPS * SSD_STATE
SSD_CONV_DIM = SSD_INNER + 2 * SSD_BC_WIDTH
D_FF = 4 * D_MODEL
OFF_POOL = D_MODEL
OFF_Z = OFF_POOL + SSD_INNER
OFF_XBC = OFF_Z + SSD_CONV_DIM
OFF_DT = OFF_XBC + SSD_HEADS
DEEPNORM_ALPHA = 2.0 ** 0.25
LN_EPS = 1e-5
RMS_EPS = 1e-5
LOG2_E = math.log2(math.e)

LANES = 128
SUBLANES = 8
VMEM_LIMIT_BYTES = 56 * 1024 * 1024
UNIT = 128
UNIT_Q = UNIT // SUBLANES
POOL_HALO = max(POOL_WINDOWS) - 1
CONV_HALO = SSD_CONV - 1
PROJ_TILE = 512
PROJ_COLS = 512
SSD_TILE = 256
MLP_TILE = 1024
MLP_COLS = 1024


def _dot(a, b):
    return jnp.dot(a, b, preferred_element_type=F32)


def _const_spec(shape):
    return pl.BlockSpec(shape, lambda *_: (0,) * len(shape), pipeline_mode=pl.Buffered(1))


def _layer_norm(v, g, b):
    mu = jnp.mean(v, axis=-1, keepdims=True)
    d = v - mu
    var = jnp.mean(d * d, axis=-1, keepdims=True)
    return d * lax.rsqrt(var + LN_EPS) * g + b


def _split2(v):
    hi = v.astype(BF16)
    lo = (v - hi.astype(F32)).astype(BF16)
    return hi, lo


def _unit_token(rows):
    return (rows & (SUBLANES - 1)) * UNIT_Q + (rows >> (SUBLANES.bit_length() - 1))


def _halo(prev_tail, cur_tail):
    n = cur_tail.shape[0] // SUBLANES
    last_sublane = lax.broadcasted_iota(jnp.int32, (SUBLANES, cur_tail.shape[1]), 0) == SUBLANES - 1
    blocks = []
    for j in range(n):
        rows = slice(j * SUBLANES, (j + 1) * SUBLANES)
        blocks.append(pltpu.roll(jnp.where(last_sublane, prev_tail[rows], cur_tail[rows]), 1, 0))
    return blocks[0] if n == 1 else jnp.concatenate(blocks, axis=0)


def _lagged(halo, unit, lag):
    if lag == 0:
        return unit
    hrows = halo.shape[0]
    return jnp.concatenate([halo[hrows - SUBLANES * lag:hrows], unit[:UNIT - SUBLANES * lag]], axis=0)


def _proj_kernel(x_ref, perm_ref, wp_ref, wz_ref, wxbc_ref, wdt_ref, wg_ref, bg_ref,
                 cw_ref, cb_ref, dtb_ref, wpg_ref, ps_ref,
                 xs_ref, bm_ref, cm_ref, dt_ref, zs_ref, gyp_ref, gb_ref,
                 utail, ctail, rbuf):
    tile = x_ref.shape[1]
    n_units = tile // UNIT
    step = pl.program_id(1)
    pool_tail = SUBLANES * POOL_HALO
    conv_tail = SUBLANES * CONV_HALO

    @pl.when(step == 0)
    def _():
        utail[...] = jnp.zeros(utail.shape, F32)
        ctail[...] = jnp.zeros(ctail.shape, F32)

    x_nat = x_ref[0].astype(BF16)
    xb = jnp.concatenate(
        [_dot(perm_ref[...], x_nat[k * UNIT:(k + 1) * UNIT]).astype(BF16) for k in range(n_units)],
        axis=0)

    rows = lax.broadcasted_iota(jnp.int32, (UNIT, 1), 0)
    unit_pos = (_unit_token(rows) + 1).astype(F32)

    def pool_epilogue(c0, u):
        outs = []
        for gi in range(c0 // POOL_GROUP_WIDTH, (c0 + u.shape[1]) // POOL_GROUP_WIDTH):
            win = POOL_WINDOWS[gi]
            cols = slice(gi * POOL_GROUP_WIDTH, (gi + 1) * POOL_GROUP_WIDTH)
            local = slice(cols.start - c0, cols.stop - c0)
            hrows = SUBLANES * (win - 1)
            pooled = []
            for k in range(n_units):
                ug = u[k * UNIT:(k + 1) * UNIT, local]
                cur_tail = ug[UNIT - hrows:]
                if k == 0:
                    prev_tail = utail[pool_tail - hrows:pool_tail, cols]
                else:
                    prev_tail = u[k * UNIT - hrows:k * UNIT, local]
                halo = _halo(prev_tail, cur_tail)
                acc = ug
                for lag in range(1, win):
                    acc = acc + _lagged(halo, ug, lag)
                pos = unit_pos + (step * tile + k * UNIT).astype(F32)
                inv_cnt = 1.0 / jnp.minimum(pos, float(win))
                pooled.append((acc * inv_cnt - ug).astype(BF16))
            pooled = jnp.concatenate(pooled, axis=0)
            outs.append(_dot(pooled, wpg_ref[gi]) * ps_ref[:, cols])
        utail[:, c0:c0 + u.shape[1]] = u[tile - pool_tail:, :]
        y_pool[c0] = jnp.concatenate(outs, axis=1)

    def conv_epilogue(c0, raw):
        cols = slice(c0, c0 + raw.shape[1])
        acts = []
        for k in range(n_units):
            unit = raw[k * UNIT:(k + 1) * UNIT]
            cur_tail = unit[UNIT - conv_tail:]
            prev_tail = ctail[:, cols] if k == 0 else raw[k * UNIT - conv_tail:k * UNIT]
            halo = _halo(prev_tail, cur_tail)
            acc = unit * cw_ref[SSD_CONV - 1:SSD_CONV, cols] + cb_ref[:, cols]
            for tap in range(SSD_CONV - 1):
                acc = acc + _lagged(halo, unit, SSD_CONV - 1 - tap) * cw_ref[tap:tap + 1, cols]
            acts.append((acc * jax.nn.sigmoid(acc)).astype(BF16))
        ctail[:, cols] = raw[tile - conv_tail:]
        act = jnp.concatenate(acts, axis=0)
        for ref, start in ((xs_ref, 0), (bm_ref, SSD_INNER), (cm_ref, SSD_INNER + SSD_BC_WIDTH)):
            if start <= c0 < start + ref.shape[2]:
                ref[0, :, c0 - start:c0 - start + raw.shape[1]] = act

    y_pool = {}

    def gate_epilogue(c0, raw):
        gate = jax.nn.sigmoid(raw + bg_ref[:, c0:c0 + raw.shape[1]])
        if c0 < D_MODEL:
            gyp_ref[0, :, c0:c0 + raw.shape[1]] = (gate * y_pool[c0]).astype(BF16)
        else:
            gb_ref[0, :, c0 - D_MODEL:c0 - D_MODEL + raw.shape[1]] = gate.astype(BF16)

    def z_epilogue(c0, raw):
        zs_ref[0, :, c0:c0 + raw.shape[1]] = (raw * jax.nn.sigmoid(raw)).astype(BF16)

    def dt_epilogue(c0, raw):
        dt_ref[0] = jax.nn.softplus(raw + dtb_ref[...])

    blocks = lambda ref, fn: [(ref, c0, PROJ_COLS, fn) for c0 in range(0, ref.shape[1], PROJ_COLS)]
    heavy = blocks(wxbc_ref, conv_epilogue)
    light = blocks(wg_ref, gate_epilogue) + blocks(wz_ref, z_epilogue)
    assert len(heavy) == len(light)
    stages = blocks(wp_ref, pool_epilogue)
    for pair in zip(heavy, light):
        stages += pair
    stages.append((wdt_ref, 0, LANES, dt_epilogue))

    runtime_zero = jnp.minimum(step, 0)

    def slot(i):
        return runtime_zero + (i % 2)

    def issue(i):
        ref, c0, width, _ = stages[i]
        rbuf[slot(i), :, 0:width] = _dot(xb, ref[:, c0:c0 + width])

    issue(0)
    for i, (_, c0, width, epilogue) in enumerate(stages):
        if i + 1 < len(stages):
            issue(i + 1)
        epilogue(c0, rbuf[slot(i), :, 0:width])


def _proj_call(x, perm, wp, wz, wxbc, wdt, wg, bg, cw, cb, dtb, wpg, ps):
    bsz, seq, _ = x.shape
    tile = min(PROJ_TILE, seq)
    assert seq % tile == 0 and tile % UNIT == 0 and POOL_HALO < UNIT_Q
    assert SSD_BC_WIDTH % PROJ_COLS == 0 and D_MODEL % PROJ_COLS == 0
    tok = lambda w: pl.BlockSpec((1, tile, w), lambda b, s: (b, s, 0))
    out_widths = (SSD_INNER, SSD_BC_WIDTH, SSD_BC_WIDTH, LANES, SSD_INNER, D_MODEL, D_MODEL)
    out_dtypes = (BF16, BF16, BF16, F32, BF16, BF16, BF16)
    consts = (perm, wp, wz, wxbc, wdt, wg, bg, cw, cb, dtb, wpg, ps)
    return pl.pallas_call(
        _proj_kernel,
        grid=(bsz, seq // tile),
        in_specs=[tok(D_MODEL)] + [_const_spec(a.shape) for a in consts],
        out_specs=[tok(w) for w in out_widths],
        out_shape=[jax.ShapeDtypeStruct((bsz, seq, w), d) for w, d in zip(out_widths, out_dtypes)],
        scratch_shapes=[pltpu.VMEM((SUBLANES * POOL_HALO, D_MODEL), F32),
                        pltpu.VMEM((SUBLANES * CONV_HALO, SSD_CONV_DIM), F32),
                        pltpu.VMEM((2, tile, PROJ_COLS), F32)],
        compiler_params=pltpu.CompilerParams(
            dimension_semantics=("arbitrary", "arbitrary"),
            vmem_limit_bytes=VMEM_LIMIT_BYTES),
        name="proj_pool_conv",
    )(x, *consts)


def _ssd_kernel(xs_ref, bm_ref, cm_ref, dt_ref, zs_ref, gyp_ref, gb_ref, x_ref,
                a_ref, dexp_ref, nw_ref, tri_ref, unperm_ref, e2_ref, wsp_ref, wout_ref,
                g1_ref, b1_ref, h1_ref, st_ref, yn_ref):
    tile = xs_ref.shape[1]
    n_units = tile // UNIT

    @pl.when(pl.program_id(1) == 0)
    def _():
        st_ref[...] = jnp.zeros(st_ref.shape, F32)

    tok_row = _unit_token(lax.broadcasted_iota(jnp.int32, (UNIT, UNIT), 0))
    tok_col = _unit_token(lax.broadcasted_iota(jnp.int32, (UNIT, UNIT), 1))
    causal = tok_row >= tok_col
    lane = lax.broadcasted_iota(jnp.int32, (UNIT, SSD_GROUP_WIDTH), 1)
    head_lanes = [(lane >= r * SSD_HEAD_DIM) & (lane < (r + 1) * SSD_HEAD_DIM)
                  for r in range(SSD_HEADS_PER_GROUP)]

    for k in range(n_units):
        rows = slice(k * UNIT, (k + 1) * UNIT)
        dt = dt_ref[0, rows, :]
        da = dt * a_ref[...]
        hi = da.astype(BF16)
        r1 = da - hi.astype(F32)
        mid = r1.astype(BF16)
        lo = (r1 - mid.astype(F32)).astype(BF16)
        cs3 = _dot(tri_ref[...], jnp.concatenate([hi, mid, lo], axis=1))
        cs = cs3[:, 0:LANES] + cs3[:, LANES:2 * LANES] + cs3[:, 2 * LANES:3 * LANES]
        last = cs[UNIT - 1:UNIT, :]
        e_cs = jnp.exp(cs)
        w_end = dt * jnp.exp(last - cs)
        cdec = jnp.broadcast_to(jnp.exp(last), (SUBLANES, LANES))
        cs2 = cs * LOG2_E
        cs2_t = cs2.T
        dt_t = dt.T

        stacked = jnp.concatenate([w_end, e_cs, cdec], axis=0)
        s_hi, s_lo = _split2(stacked)
        expanded = _dot(jnp.concatenate([s_hi, s_lo], axis=1), e2_ref[...])
        w_exp = expanded[0:UNIT]
        e_exp = expanded[UNIT:2 * UNIT]
        cdec_exp = expanded[2 * UNIT:2 * UNIT + 1]

        for g in range(SSD_GROUPS):
            scols = slice(g * SSD_STATE, (g + 1) * SSD_STATE)
            gcols = slice(g * SSD_GROUP_WIDTH, (g + 1) * SSD_GROUP_WIDTH)
            b_g = bm_ref[0, rows, scols]
            c_g = cm_ref[0, rows, scols]
            x_g = xs_ref[0, rows, gcols]
            cb = lax.dot_general(c_g, b_g, (((1,), (1,)), ((), ())), preferred_element_type=F32)
            decays, x_heads = [], []
            for r in range(SSD_HEADS_PER_GROUP):
                h = g * SSD_HEADS_PER_GROUP + r
                seg2 = cs2[:, h:h + 1] - cs2_t[h:h + 1, :]
                decay = jnp.where(causal, jnp.exp2(seg2), 0.0) * dt_t[h:h + 1, :]
                decays.append((cb * decay).astype(BF16))
                x_heads.append(jnp.where(head_lanes[r], x_g, jnp.zeros_like(x_g)))
            y_diag = _dot(jnp.concatenate(decays, axis=1), jnp.concatenate(x_heads, axis=0))
            state_t = st_ref[g]
            y_off = _dot(c_g, state_t.astype(BF16)) * e_exp[:, gcols]
            x_f = x_g.astype(F32)
            y = y_diag + y_off + dexp_ref[:, gcols] * x_f
            x_dec = (x_f * w_exp[:, gcols]).astype(BF16)
            new = lax.dot_general(b_g, x_dec, (((0,), (0,)), ((), ())), preferred_element_type=F32)
            st_ref[g] = cdec_exp[:, gcols] * state_t + new
            yg = y * zs_ref[0, rows, gcols].astype(F32)
            ms = jnp.mean(yg * yg, axis=-1, keepdims=True)
            yn_ref[rows, gcols] = (yg * lax.rsqrt(ms + RMS_EPS) * nw_ref[:, gcols]).astype(BF16)

    y_ssd = _dot(yn_ref[...], wsp_ref[...])
    merged = (gyp_ref[0].astype(F32) + gb_ref[0].astype(F32) * y_ssd).astype(BF16)
    merged = jnp.concatenate(
        [_dot(unperm_ref[...], merged[k * UNIT:(k + 1) * UNIT]).astype(BF16) for k in range(n_units)],
        axis=0)
    mix = _dot(merged, wout_ref[...])
    h1_ref[0] = _layer_norm(DEEPNORM_ALPHA * x_ref[0] + mix, g1_ref[...], b1_ref[...])


def _ssd_call(xs, bm, cm, dt, zs, gyp, gb, x, a_row, dexp, nw, tri, unperm, e2, wsp, wout, g1, b1):
    bsz, seq, _ = x.shape
    tile = min(SSD_TILE, seq)
    assert seq % tile == 0 and tile % UNIT == 0
    tok = lambda w: pl.BlockSpec((1, tile, w), lambda b, c: (b, c, 0))
    consts = (a_row, dexp, nw, tri, unperm, e2, wsp, wout, g1, b1)
    return pl.pallas_call(
        _ssd_kernel,
        grid=(bsz, seq // tile),
        in_specs=[tok(a.shape[-1]) for a in (xs, bm, cm, dt, zs, gyp, gb, x)]
                 + [_const_spec(a.shape) for a in consts],
        out_specs=tok(D_MODEL),
        out_shape=jax.ShapeDtypeStruct((bsz, seq, D_MODEL), F32),
        scratch_shapes=[pltpu.VMEM((SSD_GROUPS, SSD_STATE, SSD_GROUP_WIDTH), F32),
                        pltpu.VMEM((tile, SSD_INNER), BF16)],
        compiler_params=pltpu.CompilerParams(
            dimension_semantics=("arbitrary", "arbitrary"),
            vmem_limit_bytes=VMEM_LIMIT_BYTES),
        name="ssd_merge_ln1",
    )(xs, bm, cm, dt, zs, gyp, gb, x, *consts)


def _mlp_kernel(h_ref, wup_ref, wdn_ref, g_ref, b_ref, o_ref):
    h = h_ref[...]
    hb = h.astype(BF16)
    ff = jnp.zeros(h.shape, F32)
    for j in range(D_FF // MLP_COLS):
        cols = slice(j * MLP_COLS, (j + 1) * MLP_COLS)
        up = jnp.maximum(_dot(hb, wup_ref[:, cols]), 0.0)
        ff = ff + _dot((up * up).astype(BF16), wdn_ref[cols, :])
    o_ref[...] = _layer_norm(DEEPNORM_ALPHA * h + ff, g_ref[...], b_ref[...])


def _mlp_call(h, wup, wdn, g, b):
    rows = h.shape[0]
    tile = min(MLP_TILE, rows)
    assert rows % tile == 0
    tok = pl.BlockSpec((tile, D_MODEL), lambda i: (i, 0))
    return pl.pallas_call(
        _mlp_kernel,
        grid=(rows // tile,),
        in_specs=[tok] + [_const_spec(a.shape) for a in (wup, wdn, g, b)],
        out_specs=tok,
        out_shape=jax.ShapeDtypeStruct(h.shape, F32),
        compiler_params=pltpu.CompilerParams(
            dimension_semantics=("arbitrary",),
            vmem_limit_bytes=VMEM_LIMIT_BYTES),
        name="mlp_ln2",
    )(h, wup, wdn, g, b)


def _layer(x, w_in, b_gates, conv_w, conv_b, dt_bias, a_log, d_skip, ssd_norm_w, w_ssd_proj,
           w_pool_group, pool_scale, w_out, ln1_g, ln1_b, w_up, w_down, ln2_g, ln2_b):
    bsz, seq, _ = x.shape
    row = lambda v: v.astype(F32).reshape(1, -1)
    pad_heads = lambda v: jnp.pad(v.astype(F32), (0, LANES - SSD_HEADS)).reshape(1, LANES)

    idx = jnp.arange(UNIT)
    tok_of_row = _unit_token(idx)
    perm = (tok_of_row[:, None] == idx[None, :]).astype(BF16)
    unperm = perm.T
    tri = (tok_of_row[:, None] >= tok_of_row[None, :]).astype(BF16)
    head_of = jnp.arange(SSD_INNER) // SSD_HEAD_DIM
    e1 = (jnp.arange(LANES)[:, None] == head_of[None, :]).astype(BF16)
    e2 = jnp.concatenate([e1, e1], axis=0)

    wb = w_in.astype(BF16)
    wp, wz, wxbc = wb[:, :OFF_POOL], wb[:, OFF_POOL:OFF_Z], wb[:, OFF_Z:OFF_XBC]
    wdt = jnp.pad(wb[:, OFF_XBC:OFF_DT], ((0, 0), (0, LANES - SSD_HEADS)))
    wg = wb[:, OFF_DT:]
    xs, bm, cm, dt, zs, gyp, gb = _proj_call(
        x, perm, wp, wz, wxbc, wdt, wg, row(b_gates), conv_w.astype(F32), row(conv_b),
        pad_heads(dt_bias), w_pool_group.astype(BF16), row(pool_scale))

    a_row = pad_heads(-jnp.exp(a_log.astype(F32)))
    dexp = jnp.repeat(d_skip.astype(F32), SSD_HEAD_DIM).reshape(1, SSD_INNER)
    h1 = _ssd_call(xs, bm, cm, dt, zs, gyp, gb, x, a_row, dexp, row(ssd_norm_w), tri, unperm, e2,
                   w_ssd_proj.astype(BF16), w_out.astype(BF16), row(ln1_g), row(ln1_b))

    h2 = _mlp_call(h1.reshape(bsz * seq, D_MODEL), w_up.astype(BF16), w_down.astype(BF16),
                   row(ln2_g), row(ln2_b))
    return h2.reshape(bsz, seq, D_MODEL)


def kernel(x, w_in, b_gates, conv_w, conv_b, dt_bias, a_log, d_skip, ssd_norm_w, w_ssd_proj,
           w_pool_group, pool_scale, w_out, ln1_g, ln1_b, w_up, w_down, ln2_g, ln2_b):
    h = x
    for layer in range(w_in.shape[0]):
        h = _layer(h, w_in[layer], b_gates[layer], conv_w[layer], conv_b[layer], dt_bias[layer],
                   a_log[layer], d_skip[layer], ssd_norm_w[layer], w_ssd_proj[layer],
                   w_pool_group[layer], pool_scale[layer], w_out[layer], ln1_g[layer],
                   ln1_b[layer], w_up[layer], w_down[layer], ln2_g[layer], ln2_b[layer])
    return h
```

```python
import math

import jax
import jax.numpy as jnp
from jax import lax
from jax.experimental import pallas as pl
from jax.experimental.pallas import tpu as pltpu

F32 = jnp.float32
BF16 = jnp.bfloat16

D_MODEL = 1024
POOL_WINDOWS = (2, 4, 8, 16)
POOL_GROUP_WIDTH = D_MODEL // len(POOL_WINDOWS)
SSD_INNER = 2 * D_MODEL
SSD_HEAD_DIM = 64
SSD_HEADS = SSD_INNER // SSD_HEAD_DIM
SSD_GROUPS = 8
SSD_HEADS_PER_GROUP = SSD_HEADS // SSD_GROUPS
SSD_GROUP_WIDTH = SSD_INNER // SSD_GROUPS
SSD_STATE = 128
SSD_CONV = 4
SSD_BC_WIDTH = SSD_GROUPS * SSD_STATE
SSD_CONV_DIM = SSD_INNER + 2 * SSD_BC_WIDTH
D_FF = 4 * D_MODEL
OFF_POOL = D_MODEL
OFF_Z = OFF_POOL + SSD_INNER
OFF_XBC = OFF_Z + SSD_CONV_DIM
OFF_DT = OFF_XBC + SSD_HEADS
DEEPNORM_ALPHA = 2.0 ** 0.25
LN_EPS = 1e-5
RMS_EPS = 1e-5
LOG2_E = math.log2(math.e)

LANES = 128
SUBLANES = 8
BF16_ROWS = 16
VMEM_LIMIT_BYTES = 56 * 1024 * 1024
UNIT = 128
UNIT_Q = UNIT // SUBLANES
POOL_HALO = max(POOL_WINDOWS) - 1
CONV_HALO = SSD_CONV - 1
MIX_TILE = 512
PROJ_COLS = 512
SCALE_ROWS = 2 * UNIT + BF16_ROWS
MERGE_TILE = 1024
MLP_TILE = 1024
MLP_COLS = 1024
STAGE_VALU_BUDGET = 4800
SSD_PIECE_VALU = 1100
SSD_PREP_VALU = 1500
EPILOGUE_VALU = {"conv": 3500, "pool": 2500, "gate": 1500, "z": 1500, "dt": 300}


def _dot(a, b):
    return jnp.dot(a, b, preferred_element_type=F32)


def _const_spec(shape):
    return pl.BlockSpec(shape, lambda *_: (0,) * len(shape), pipeline_mode=pl.Buffered(1))


def _layer_norm(v, g, b):
    mu = jnp.mean(v, axis=-1, keepdims=True)
    d = v - mu
    var = jnp.mean(d * d, axis=-1, keepdims=True)
    return d * lax.rsqrt(var + LN_EPS) * g + b


def _split2(v):
    hi = v.astype(BF16)
    lo = (v - hi.astype(F32)).astype(BF16)
    return hi, lo


def _unit_token(rows):
    return (rows & (SUBLANES - 1)) * UNIT_Q + (rows >> (SUBLANES.bit_length() - 1))


def _halo(prev_tail, cur_tail):
    n = cur_tail.shape[0] // SUBLANES
    last_sublane = lax.broadcasted_iota(jnp.int32, (SUBLANES, cur_tail.shape[1]), 0) == SUBLANES - 1
    blocks = []
    for j in range(n):
        rows = slice(j * SUBLANES, (j + 1) * SUBLANES)
        blocks.append(pltpu.roll(jnp.where(last_sublane, prev_tail[rows], cur_tail[rows]), 1, 0))
    return blocks[0] if n == 1 else jnp.concatenate(blocks, axis=0)


def _lagged(halo, unit, lag):
    if lag == 0:
        return unit
    hrows = halo.shape[0]
    return jnp.concatenate([halo[hrows - SUBLANES * lag:hrows], unit[:UNIT - SUBLANES * lag]], axis=0)


def _mixer_kernel(x_ref, perm_ref, wp_ref, wz_ref, wxbc_ref, wdt_ref, wg_ref, bg_ref,
                  cw_ref, cb_ref, dtb_ref, wpg_ref, ps_ref,
                  a_ref, dexp_ref, nw_ref, tri_ref, e2_ref,
                  yn_ref, gyp_ref, gb_ref,
                  utail, ctail, rbuf, xs_s, bm_s, cm_s, zs_s, dt_s, heads_s, scales_s, st_ref):
    tile = x_ref.shape[1]
    n_units = tile // UNIT
    step = pl.program_id(1)
    pool_tail = SUBLANES * POOL_HALO
    conv_tail = SUBLANES * CONV_HALO

    @pl.when(step == 0)
    def _():
        utail[...] = jnp.zeros(utail.shape, F32)
        ctail[...] = jnp.zeros(ctail.shape, F32)
        st_ref[...] = jnp.zeros(st_ref.shape, F32)

    x_nat = x_ref[0].astype(BF16)
    xb = jnp.concatenate(
        [_dot(perm_ref[...], x_nat[k * UNIT:(k + 1) * UNIT]).astype(BF16) for k in range(n_units)],
        axis=0)

    rows = lax.broadcasted_iota(jnp.int32, (UNIT, 1), 0)
    unit_pos = (_unit_token(rows) + 1).astype(F32)
    y_pool = {}

    def pool_epilogue(c0, u):
        outs = []
        for gi in range(c0 // POOL_GROUP_WIDTH, (c0 + u.shape[1]) // POOL_GROUP_WIDTH):
            win = POOL_WINDOWS[gi]
            cols = slice(gi * POOL_GROUP_WIDTH, (gi + 1) * POOL_GROUP_WIDTH)
            local = slice(cols.start - c0, cols.stop - c0)
            hrows = SUBLANES * (win - 1)
            pooled = []
            for k in range(n_units):
                ug = u[k * UNIT:(k + 1) * UNIT, local]
                cur_tail = ug[UNIT - hrows:]
                if k == 0:
                    prev_tail = utail[pool_tail - hrows:pool_tail, cols]
                else:
                    prev_tail = u[k * UNIT - hrows:k * UNIT, local]
                halo = _halo(prev_tail, cur_tail)
                acc = ug
                for lag in range(1, win):
                    acc = acc + _lagged(halo, ug, lag)
                pos = unit_pos + (step * tile + k * UNIT).astype(F32)
                inv_cnt = 1.0 / jnp.minimum(pos, float(win))
                pooled.append((acc * inv_cnt - ug).astype(BF16))
            pooled = jnp.concatenate(pooled, axis=0)
            outs.append(_dot(pooled, wpg_ref[gi]) * ps_ref[:, cols])
        utail[:, c0:c0 + u.shape[1]] = u[tile - pool_tail:, :]
        y_pool[c0] = jnp.concatenate(outs, axis=1)

    def conv_epilogue(c0, raw):
        cols = slice(c0, c0 + raw.shape[1])
        acts = []
        for k in range(n_units):
            unit = raw[k * UNIT:(k + 1) * UNIT]
            cur_tail = unit[UNIT - conv_tail:]
            prev_tail = ctail[:, cols] if k == 0 else raw[k * UNIT - conv_tail:k * UNIT]
            halo = _halo(prev_tail, cur_tail)
            acc = unit * cw_ref[SSD_CONV - 1:SSD_CONV, cols] + cb_ref[:, cols]
            for tap in range(SSD_CONV - 1):
                acc = acc + _lagged(halo, unit, SSD_CONV - 1 - tap) * cw_ref[tap:tap + 1, cols]
            acts.append((acc * jax.nn.sigmoid(acc)).astype(BF16))
        ctail[:, cols] = raw[tile - conv_tail:]
        act = jnp.concatenate(acts, axis=0)
        for ref, start in ((xs_s, 0), (bm_s, SSD_INNER), (cm_s, SSD_INNER + SSD_BC_WIDTH)):
            if start <= c0 < start + ref.shape[1]:
                ref[:, c0 - start:c0 - start + raw.shape[1]] = act

    def gate_epilogue(c0, raw):
        gate = jax.nn.sigmoid(raw + bg_ref[:, c0:c0 + raw.shape[1]])
        if c0 < D_MODEL:
            gyp_ref[0, :, c0:c0 + raw.shape[1]] = (gate * y_pool[c0]).astype(BF16)
        else:
            gb_ref[0, :, c0 - D_MODEL:c0 - D_MODEL + raw.shape[1]] = gate.astype(BF16)

    def z_epilogue(c0, raw):
        zs_s[:, c0:c0 + raw.shape[1]] = (raw * jax.nn.sigmoid(raw)).astype(BF16)

    def dt_epilogue(c0, raw):
        dt_s[...] = jax.nn.softplus(raw + dtb_ref[...])

    tok_row = _unit_token(lax.broadcasted_iota(jnp.int32, (UNIT, UNIT), 0))
    tok_col = _unit_token(lax.broadcasted_iota(jnp.int32, (UNIT, UNIT), 1))
    causal = tok_row >= tok_col
    lane = lax.broadcasted_iota(jnp.int32, (UNIT, SSD_GROUP_WIDTH), 1)
    head_lanes = [(lane >= r * SSD_HEAD_DIM) & (lane < (r + 1) * SSD_HEAD_DIM)
                  for r in range(SSD_HEADS_PER_GROUP)]

    def ssd_prep(k):
        dt = dt_s[k * UNIT:(k + 1) * UNIT, :]
        da = dt * a_ref[...]
        hi = da.astype(BF16)
        r1 = da - hi.astype(F32)
        mid = r1.astype(BF16)
        lo = (r1 - mid.astype(F32)).astype(BF16)
        cs3 = _dot(tri_ref[...], jnp.concatenate([hi, mid, lo], axis=1))
        cs = cs3[:, 0:LANES] + cs3[:, LANES:2 * LANES] + cs3[:, 2 * LANES:3 * LANES]
        last = cs[UNIT - 1:UNIT, :]
        cs2 = cs * LOG2_E
        heads_s[k, 0] = cs2
        heads_s[k, 1] = cs2.T
        heads_s[k, 2] = dt.T
        stacked = jnp.concatenate(
            [dt * jnp.exp(last - cs),
             jnp.exp(cs),
             jnp.broadcast_to(jnp.exp(last), (BF16_ROWS, LANES))],
            axis=0)
        s_hi, s_lo = _split2(stacked)
        scales_s[k] = jnp.concatenate([s_hi, s_lo], axis=1)

    def ssd_piece(k, g):
        rows_k = slice(k * UNIT, (k + 1) * UNIT)
        scols = slice(g * SSD_STATE, (g + 1) * SSD_STATE)
        gcols = slice(g * SSD_GROUP_WIDTH, (g + 1) * SSD_GROUP_WIDTH)
        cs2, cs2_t, dt_t = heads_s[k, 0], heads_s[k, 1], heads_s[k, 2]
        expanded = _dot(scales_s[k], e2_ref[:, gcols])
        w_exp = expanded[0:UNIT]
        e_exp = expanded[UNIT:2 * UNIT]
        cdec_exp = expanded[2 * UNIT:2 * UNIT + 1]
        b_g = bm_s[rows_k, scols]
        c_g = cm_s[rows_k, scols]
        x_g = xs_s[rows_k, gcols]
        cb = lax.dot_general(c_g, b_g, (((1,), (1,)), ((), ())), preferred_element_type=F32)
        decays, x_heads = [], []
        for r in range(SSD_HEADS_PER_GROUP):
            h = g * SSD_HEADS_PER_GROUP + r
            seg2 = cs2[:, h:h + 1] - cs2_t[h:h + 1, :]
            decay = jnp.where(causal, jnp.exp2(seg2), 0.0) * dt_t[h:h + 1, :]
            decays.append((cb * decay).astype(BF16))
            x_heads.append(jnp.where(head_lanes[r], x_g, jnp.zeros_like(x_g)))
        y_diag = _dot(jnp.concatenate(decays, axis=1), jnp.concatenate(x_heads, axis=0))
        state_t = st_ref[g]
        y_off = _dot(c_g, state_t.astype(BF16)) * e_exp
        x_f = x_g.astype(F32)
        y = y_diag + y_off + dexp_ref[:, gcols] * x_f
        x_dec = (x_f * w_exp).astype(BF16)
        new = lax.dot_general(b_g, x_dec, (((0,), (0,)), ((), ())), preferred_element_type=F32)
        st_ref[g] = cdec_exp * state_t + new
        yg = y * zs_s[rows_k, gcols].astype(F32)
        ms = jnp.mean(yg * yg, axis=-1, keepdims=True)
        yn_ref[0, rows_k, gcols] = (yg * lax.rsqrt(ms + RMS_EPS) * nw_ref[:, gcols]).astype(BF16)

    def blocks(ref, fn, kind, lo=0, hi=None):
        hi = ref.shape[1] if hi is None else hi
        return [(ref, c0, PROJ_COLS, fn, kind) for c0 in range(lo, hi, PROJ_COLS)]

    bc = blocks(wxbc_ref, conv_epilogue, "conv", SSD_INNER, SSD_CONV_DIM)
    xs_blocks = blocks(wxbc_ref, conv_epilogue, "conv", 0, SSD_INNER)
    z_blocks = blocks(wz_ref, z_epilogue, "z")
    assert len(xs_blocks) == len(z_blocks)
    stages = [(wdt_ref, 0, LANES, dt_epilogue, "dt")] + bc
    ready_after = {}
    for xs_blk, z_blk in zip(xs_blocks, z_blocks):
        stages += [xs_blk, z_blk]
        for g in range(xs_blk[1] // SSD_GROUP_WIDTH, (xs_blk[1] + PROJ_COLS) // SSD_GROUP_WIDTH):
            ready_after[g] = len(stages) - 1
    stages += blocks(wp_ref, pool_epilogue, "pool") + blocks(wg_ref, gate_epilogue, "gate")

    extra = [[] for _ in stages]
    preps = list(range(n_units))
    next_unit = {g: 0 for g in range(SSD_GROUPS)}
    for i, stage in enumerate(stages):
        load = EPILOGUE_VALU[stage[4]]
        while preps and i >= 1 and load < STAGE_VALU_BUDGET:
            extra[i].append((ssd_prep, (preps.pop(0),)))
            load += SSD_PREP_VALU
        last = i == len(stages) - 1
        progressed = True
        while progressed and (last or load < STAGE_VALU_BUDGET):
            progressed = False
            for g in range(SSD_GROUPS):
                if ready_after[g] <= i and next_unit[g] < n_units and not preps \
                        and (last or load < STAGE_VALU_BUDGET):
                    extra[i].append((ssd_piece, (next_unit[g], g)))
                    next_unit[g] += 1
                    load += SSD_PIECE_VALU
                    progressed = True
    assert not preps and all(u == n_units for u in next_unit.values())

    runtime_zero = jnp.minimum(step, 0)

    def slot(i):
        return runtime_zero + (i % 2)

    def issue(i):
        ref, c0, width = stages[i][:3]
        rbuf[slot(i), :, 0:width] = _dot(xb, ref[:, c0:c0 + width])

    issue(0)
    for i, (_, c0, width, epilogue, _) in enumerate(stages):
        if i + 1 < len(stages):
            issue(i + 1)
        epilogue(c0, rbuf[slot(i), :, 0:width])
        for fn, args in extra[i]:
            fn(*args)


def _mixer_call(x, consts):
    bsz, seq, _ = x.shape
    tile = min(MIX_TILE, seq)
    n_units = tile // UNIT
    assert seq % tile == 0 and tile % UNIT == 0 and POOL_HALO < UNIT_Q
    assert SSD_BC_WIDTH % PROJ_COLS == 0 and D_MODEL % PROJ_COLS == 0
    assert PROJ_COLS % SSD_GROUP_WIDTH == 0
    tok = lambda w: pl.BlockSpec((1, tile, w), lambda b, s: (b, s, 0))
    out_widths = (SSD_INNER, D_MODEL, D_MODEL)
    return pl.pallas_call(
        _mixer_kernel,
        grid=(bsz, seq // tile),
        in_specs=[tok(D_MODEL)] + [_const_spec(a.shape) for a in consts],
        out_specs=[tok(w) for w in out_widths],
        out_shape=[jax.ShapeDtypeStruct((bsz, seq, w), BF16) for w in out_widths],
        scratch_shapes=[pltpu.VMEM((SUBLANES * POOL_HALO, D_MODEL), F32),
                        pltpu.VMEM((SUBLANES * CONV_HALO, SSD_CONV_DIM), F32),
                        pltpu.VMEM((2, tile, PROJ_COLS), F32),
                        pltpu.VMEM((tile, SSD_INNER), BF16),
                        pltpu.VMEM((tile, SSD_BC_WIDTH), BF16),
                        pltpu.VMEM((tile, SSD_BC_WIDTH), BF16),
                        pltpu.VMEM((tile, SSD_INNER), BF16),
                        pltpu.VMEM((tile, LANES), F32),
                        pltpu.VMEM((n_units, 3, UNIT, LANES), F32),
                        pltpu.VMEM((n_units, SCALE_ROWS, 2 * LANES), BF16),
                        pltpu.VMEM((SSD_GROUPS, SSD_STATE, SSD_GROUP_WIDTH), F32)],
        compiler_params=pltpu.CompilerParams(
            dimension_semantics=("arbitrary", "arbitrary"),
            vmem_limit_bytes=VMEM_LIMIT_BYTES),
        name="token_mixers",
    )(x, *consts)


def _merge_kernel(yn_ref, gyp_ref, gb_ref, x_ref, unperm_ref, wsp_ref, wout_ref, g1_ref, b1_ref,
                  h1_ref):
    n_units = yn_ref.shape[0] // UNIT
    y_ssd = _dot(yn_ref[...], wsp_ref[...])
    merged = (gyp_ref[...].astype(F32) + gb_ref[...].astype(F32) * y_ssd).astype(BF16)
    merged = jnp.concatenate(
        [_dot(unperm_ref[...], merged[k * UNIT:(k + 1) * UNIT]).astype(BF16) for k in range(n_units)],
        axis=0)
    mix = _dot(merged, wout_ref[...])
    h1_ref[...] = _layer_norm(DEEPNORM_ALPHA * x_ref[...] + mix, g1_ref[...], b1_ref[...])


def _merge_call(yn, gyp, gb, x, consts):
    rows = x.shape[0]
    tile = min(MERGE_TILE, rows)
    assert rows % tile == 0 and tile % UNIT == 0
    tok = lambda w: pl.BlockSpec((tile, w), lambda i: (i, 0))
    return pl.pallas_call(
        _merge_kernel,
        grid=(rows // tile,),
        in_specs=[tok(a.shape[-1]) for a in (yn, gyp, gb, x)] + [_const_spec(a.shape) for a in consts],
        out_specs=tok(D_MODEL),
        out_shape=jax.ShapeDtypeStruct((rows, D_MODEL), F32),
        compiler_params=pltpu.CompilerParams(
            dimension_semantics=("arbitrary",),
            vmem_limit_bytes=VMEM_LIMIT_BYTES),
        name="merge_ln1",
    )(yn, gyp, gb, x, *consts)


def _mlp_kernel(h_ref, wup_ref, wdn_ref, g_ref, b_ref, o_ref):
    h = h_ref[...]
    hb = h.astype(BF16)
    ff = jnp.zeros(h.shape, F32)
    for j in range(D_FF // MLP_COLS):
        cols = slice(j * MLP_COLS, (j + 1) * MLP_COLS)
        up = jnp.maximum(_dot(hb, wup_ref[:, cols]), 0.0)
        ff = ff + _dot((up * up).astype(BF16), wdn_ref[cols, :])
    o_ref[...] = _layer_norm(DEEPNORM_ALPHA * h + ff, g_ref[...], b_ref[...])


def _mlp_call(h, wup, wdn, g, b):
    rows = h.shape[0]
    tile = min(MLP_TILE, rows)
    assert rows % tile == 0
    tok = pl.BlockSpec((tile, D_MODEL), lambda i: (i, 0))
    return pl.pallas_call(
        _mlp_kernel,
        grid=(rows // tile,),
        in_specs=[tok] + [_const_spec(a.shape) for a in (wup, wdn, g, b)],
        out_specs=tok,
        out_shape=jax.ShapeDtypeStruct(h.shape, F32),
        compiler_params=pltpu.CompilerParams(
            dimension_semantics=("arbitrary",),
            vmem_limit_bytes=VMEM_LIMIT_BYTES),
        name="mlp_ln2",
    )(h, wup, wdn, g, b)


def _layer(x, w_in, b_gates, conv_w, conv_b, dt_bias, a_log, d_skip, ssd_norm_w, w_ssd_proj,
           w_pool_group, pool_scale, w_out, ln1_g, ln1_b, w_up, w_down, ln2_g, ln2_b):
    bsz, seq, _ = x.shape
    row = lambda v: v.astype(F32).reshape(1, -1)
    pad_heads = lambda v: jnp.pad(v.astype(F32), (0, LANES - SSD_HEADS)).reshape(1, LANES)

    idx = jnp.arange(UNIT)
    tok_of_row = _unit_token(idx)
    perm = (tok_of_row[:, None] == idx[None, :]).astype(BF16)
    unperm = perm.T
    tri = (tok_of_row[:, None] >= tok_of_row[None, :]).astype(BF16)
    head_of = jnp.arange(SSD_INNER) // SSD_HEAD_DIM
    e1 = (jnp.arange(LANES)[:, None] == head_of[None, :]).astype(BF16)
    e2 = jnp.concatenate([e1, e1], axis=0)

    wb = w_in.astype(BF16)
    wp, wz, wxbc = wb[:, :OFF_POOL], wb[:, OFF_POOL:OFF_Z], wb[:, OFF_Z:OFF_XBC]
    wdt = jnp.pad(wb[:, OFF_XBC:OFF_DT], ((0, 0), (0, LANES - SSD_HEADS)))
    wg = wb[:, OFF_DT:]
    a_row = pad_heads(-jnp.exp(a_log.astype(F32)))
    dexp = jnp.repeat(d_skip.astype(F32), SSD_HEAD_DIM).reshape(1, SSD_INNER)
    yn, gyp, gb = _mixer_call(
        x, (perm, wp, wz, wxbc, wdt, wg, row(b_gates), conv_w.astype(F32), row(conv_b),
            pad_heads(dt_bias), w_pool_group.astype(BF16), row(pool_scale),
            a_row, dexp, row(ssd_norm_w), tri, e2))

    flat = lambda v: v.reshape(bsz * seq, v.shape[-1])
    h1 = _merge_call(flat(yn), flat(gyp), flat(gb), flat(x),
                     (unperm, w_ssd_proj.astype(BF16), w_out.astype(BF16), row(ln1_g), row(ln1_b)))
    h2 = _mlp_call(h1, w_up.astype(BF16), w_down.astype(BF16), row(ln2_g), row(ln2_b))
    return h2.reshape(bsz, seq, D_MODEL)


def kernel(x, w_in, b_gates, conv_w, conv_b, dt_bias, a_log, d_skip, ssd_norm_w, w_ssd_proj,
           w_pool_group, pool_scale, w_out, ln1_g, ln1_b, w_up, w_down, ln2_g, ln2_b):
    h = x
    for layer in range(w_in.shape[0]):
        h = _layer(h, w_in[layer], b_gates[layer], conv_w[layer], conv_b[layer], dt_bias[layer],
                   a_log[layer], d_skip[layer], ssd_norm_w[layer], w_ssd_proj[layer],
                   w_pool_group[layer], pool_scale[layer], w_out[layer], ln1_g[layer],
                   ln1_b[layer], w_up[layer], w_down[layer], ln2_g[layer], ln2_b[layer])
    return h
```

```python
import functools
import math

import jax
import jax.numpy as jnp
from jax import lax
from jax.experimental import pallas as pl
from jax.experimental.pallas import tpu as pltpu

F32 = jnp.float32
BF16 = jnp.bfloat16

D_MODEL = 1024
POOL_WINDOWS = (2, 4, 8, 16)
POOL_GROUP_WIDTH = D_MODEL // len(POOL_WINDOWS)
SSD_INNER = 2 * D_MODEL
SSD_HEAD_DIM = 64
SSD_HEADS = SSD_INNER // SSD_HEAD_DIM
SSD_GROUPS = 8
SSD_HEADS_PER_GROUP = SSD_HEADS // SSD_GROUPS
SSD_GROUP_WIDTH = SSD_INNER // SSD_GROUPS
SSD_STATE = 128
SSD_CONV = 4
SSD_BC_WIDTH = SSD_GROUPS * SSD_STATE
SSD_CONV_DIM = SSD_INNER + 2 * SSD_BC_WIDTH
D_FF = 4 * D_MODEL
OFF_POOL = D_MODEL
OFF_Z = OFF_POOL + SSD_INNER
OFF_XBC = OFF_Z + SSD_CONV_DIM
OFF_DT = OFF_XBC + SSD_HEADS
DEEPNORM_ALPHA = 2.0 ** 0.25
LN_EPS = 1e-5
RMS_EPS = 1e-5
LOG2_E = math.log2(math.e)

LANES = 128
SUBLANES = 8
VMEM_LIMIT_BYTES = 56 * 1024 * 1024
UNIT = 128
UNIT_Q = UNIT // SUBLANES
POOL_HALO = max(POOL_WINDOWS) - 1
CONV_HALO = SSD_CONV - 1
PROJ_TILE = 512
PROJ_COLS = 512
PARK_SLOTS = 3
SSD_TILE = 512
MERGE_COLS = 256
MERGE_ROWS = 256
MLP_TILE = 1024
MLP_COLS = 1024


def _dot(a, b):
    return jnp.dot(a, b, preferred_element_type=F32)


def _const_spec(shape):
    return pl.BlockSpec(shape, lambda *_: (0,) * len(shape), pipeline_mode=pl.Buffered(1))


def _layer_norm(v, g, b):
    mu = jnp.mean(v, axis=-1, keepdims=True)
    d = v - mu
    var = jnp.mean(d * d, axis=-1, keepdims=True)
    return d * lax.rsqrt(var + LN_EPS) * g + b


def _split2(v):
    hi = v.astype(BF16)
    lo = (v - hi.astype(F32)).astype(BF16)
    return hi, lo


def _unit_token(rows):
    return (rows & (SUBLANES - 1)) * UNIT_Q + (rows >> (SUBLANES.bit_length() - 1))


def _halo(prev_tail, cur_tail):
    n = cur_tail.shape[0] // SUBLANES
    last_sublane = lax.broadcasted_iota(jnp.int32, (SUBLANES, cur_tail.shape[1]), 0) == SUBLANES - 1
    blocks = []
    for j in range(n):
        rows = slice(j * SUBLANES, (j + 1) * SUBLANES)
        blocks.append(pltpu.roll(jnp.where(last_sublane, prev_tail[rows], cur_tail[rows]), 1, 0))
    return blocks[0] if n == 1 else jnp.concatenate(blocks, axis=0)


def _lagged(halo, unit, lag):
    if lag == 0:
        return unit
    hrows = halo.shape[0]
    return jnp.concatenate([halo[hrows - SUBLANES * lag:hrows], unit[:UNIT - SUBLANES * lag]], axis=0)


def _proj_kernel(x_ref, perm_ref, wp_ref, wz_ref, wxbc_ref, wdt_ref, wg_ref, bg_ref,
                 cw_ref, cb_ref, dtb_ref, wpg_ref, ps_ref,
                 xs_ref, bm_ref, cm_ref, dt_ref, zs_ref, gyp_ref, gb_ref,
                 utail, ctail, *parked):
    tile = x_ref.shape[1]
    n_units = tile // UNIT
    step = pl.program_id(1)
    pool_tail = SUBLANES * POOL_HALO
    conv_tail = SUBLANES * CONV_HALO

    @pl.when(step == 0)
    def _():
        utail[...] = jnp.zeros(utail.shape, F32)
        ctail[...] = jnp.zeros(ctail.shape, F32)

    x_nat = x_ref[0].astype(BF16)
    xb = jnp.concatenate(
        [_dot(perm_ref[...], x_nat[k * UNIT:(k + 1) * UNIT]).astype(BF16) for k in range(n_units)],
        axis=0)

    rows = lax.broadcasted_iota(jnp.int32, (UNIT, 1), 0)
    unit_pos = (_unit_token(rows) + 1).astype(F32)
    y_pool = {}

    def pool_epilogue(c0, u):
        outs = []
        for gi in range(c0 // POOL_GROUP_WIDTH, (c0 + u.shape[1]) // POOL_GROUP_WIDTH):
            win = POOL_WINDOWS[gi]
            cols = slice(gi * POOL_GROUP_WIDTH, (gi + 1) * POOL_GROUP_WIDTH)
            local = slice(cols.start - c0, cols.stop - c0)
            hrows = SUBLANES * (win - 1)
            pooled = []
            for k in range(n_units):
                ug = u[k * UNIT:(k + 1) * UNIT, local]
                cur_tail = ug[UNIT - hrows:]
                if k == 0:
                    prev_tail = utail[pool_tail - hrows:pool_tail, cols]
                else:
                    prev_tail = u[k * UNIT - hrows:k * UNIT, local]
                halo = _halo(prev_tail, cur_tail)
                acc = ug
                for lag in range(1, win):
                    acc = acc + _lagged(halo, ug, lag)
                pos = unit_pos + (step * tile + k * UNIT).astype(F32)
                inv_cnt = 1.0 / jnp.minimum(pos, float(win))
                pooled.append((acc * inv_cnt - ug).astype(BF16))
            pooled = jnp.concatenate(pooled, axis=0)
            outs.append(_dot(pooled, wpg_ref[gi]) * ps_ref[:, cols])
        utail[:, c0:c0 + u.shape[1]] = u[tile - pool_tail:, :]
        y_pool[c0] = jnp.concatenate(outs, axis=1)

    def conv_epilogue(c0, raw):
        cols = slice(c0, c0 + raw.shape[1])
        acts = []
        for k in range(n_units):
            unit = raw[k * UNIT:(k + 1) * UNIT]
            cur_tail = unit[UNIT - conv_tail:]
            prev_tail = ctail[:, cols] if k == 0 else raw[k * UNIT - conv_tail:k * UNIT]
            halo = _halo(prev_tail, cur_tail)
            acc = unit * cw_ref[SSD_CONV - 1:SSD_CONV, cols] + cb_ref[:, cols]
            for tap in range(SSD_CONV - 1):
                acc = acc + _lagged(halo, unit, SSD_CONV - 1 - tap) * cw_ref[tap:tap + 1, cols]
            acts.append((acc * jax.nn.sigmoid(acc)).astype(BF16))
        ctail[:, cols] = raw[tile - conv_tail:]
        act = jnp.concatenate(acts, axis=0)
        for ref, start in ((xs_ref, 0), (bm_ref, SSD_INNER), (cm_ref, SSD_INNER + SSD_BC_WIDTH)):
            if start <= c0 < start + ref.shape[2]:
                ref[0, :, c0 - start:c0 - start + raw.shape[1]] = act

    def gate_epilogue(c0, raw):
        gate = jax.nn.sigmoid(raw + bg_ref[:, c0:c0 + raw.shape[1]])
        if c0 < D_MODEL:
            gyp_ref[0, :, c0:c0 + raw.shape[1]] = (gate * y_pool[c0]).astype(BF16)
        else:
            gb_ref[0, :, c0 - D_MODEL:c0 - D_MODEL + raw.shape[1]] = gate.astype(BF16)

    def z_epilogue(c0, raw):
        zs_ref[0, :, c0:c0 + raw.shape[1]] = (raw * jax.nn.sigmoid(raw)).astype(BF16)

    def dt_epilogue(c0, raw):
        dt_ref[0] = jax.nn.softplus(raw + dtb_ref[...])

    blocks = lambda ref, fn: [(ref, c0, PROJ_COLS, fn) for c0 in range(0, ref.shape[1], PROJ_COLS)]
    heavy = blocks(wxbc_ref, conv_epilogue)
    light = blocks(wg_ref, gate_epilogue) + blocks(wz_ref, z_epilogue)
    assert len(heavy) == len(light)
    stages = blocks(wp_ref, pool_epilogue)
    for pair in zip(heavy, light):
        stages += pair
    stages.append((wdt_ref, 0, LANES, dt_epilogue))

    runtime_zero = jnp.minimum(step, 0)

    def issue(i):
        ref, c0, width, _ = stages[i]
        parked[i % len(parked)][runtime_zero, :, 0:width] = _dot(xb, ref[:, c0:c0 + width])

    issue(0)
    for i, (_, c0, width, epilogue) in enumerate(stages):
        if i + 1 < len(stages):
            issue(i + 1)
        epilogue(c0, parked[i % len(parked)][runtime_zero, :, 0:width])


def _proj_call(x, perm, wp, wz, wxbc, wdt, wg, bg, cw, cb, dtb, wpg, ps):
    bsz, seq, _ = x.shape
    tile = min(PROJ_TILE, seq)
    assert seq % tile == 0 and tile % UNIT == 0 and POOL_HALO < UNIT_Q
    assert SSD_BC_WIDTH % PROJ_COLS == 0 and D_MODEL % PROJ_COLS == 0
    tok = lambda w: pl.BlockSpec((1, tile, w), lambda b, s: (b, s, 0))
    out_widths = (SSD_INNER, SSD_BC_WIDTH, SSD_BC_WIDTH, LANES, SSD_INNER, D_MODEL, D_MODEL)
    out_dtypes = (BF16, BF16, BF16, F32, BF16, BF16, BF16)
    consts = (perm, wp, wz, wxbc, wdt, wg, bg, cw, cb, dtb, wpg, ps)
    return pl.pallas_call(
        _proj_kernel,
        grid=(bsz, seq // tile),
        in_specs=[tok(D_MODEL)] + [_const_spec(a.shape) for a in consts],
        out_specs=[tok(w) for w in out_widths],
        out_shape=[jax.ShapeDtypeStruct((bsz, seq, w), d) for w, d in zip(out_widths, out_dtypes)],
        scratch_shapes=[pltpu.VMEM((SUBLANES * POOL_HALO, D_MODEL), F32),
                        pltpu.VMEM((SUBLANES * CONV_HALO, SSD_CONV_DIM), F32)]
                       + [pltpu.VMEM((1, tile, PROJ_COLS), F32) for _ in range(PARK_SLOTS)],
        compiler_params=pltpu.CompilerParams(
            dimension_semantics=("arbitrary", "arbitrary"),
            vmem_limit_bytes=VMEM_LIMIT_BYTES),
        name="proj_pool_conv",
    )(x, *consts)


def _ssd_kernel(tiles_per_seq,
                xs_ref, bm_ref, cm_ref, dt_ref, zs_ref, gyp_ref, gb_ref, x_ref,
                a_ref, dexp_ref, nw_ref, tri_ref, unperm_ref, e2_ref, wsp_ref, wout_ref,
                g1_ref, b1_ref, h1_ref, st_ref, yn_ref, merged_s, ordered_s, resid_s):
    tile = xs_ref.shape[1]
    n_units = tile // UNIT
    t = pl.program_id(0)
    cur = t % 2
    prev = 1 - cur

    @pl.when(t % tiles_per_seq == 0)
    def _():
        st_ref[...] = jnp.zeros(st_ref.shape, F32)

    @pl.when(t == 0)
    def _():
        yn_ref[...] = jnp.zeros(yn_ref.shape, BF16)

    tok_row = _unit_token(lax.broadcasted_iota(jnp.int32, (UNIT, UNIT), 0))
    tok_col = _unit_token(lax.broadcasted_iota(jnp.int32, (UNIT, UNIT), 1))
    causal = tok_row >= tok_col
    lane = lax.broadcasted_iota(jnp.int32, (UNIT, SSD_GROUP_WIDTH), 1)
    head_lanes = [(lane >= r * SSD_HEAD_DIM) & (lane < (r + 1) * SSD_HEAD_DIM)
                  for r in range(SSD_HEADS_PER_GROUP)]
    unit_vals = {}

    def scan_prep(k):
        rows = slice(k * UNIT, (k + 1) * UNIT)
        dt = dt_ref[0, rows, :]
        da = dt * a_ref[...]
        hi = da.astype(BF16)
        r1 = da - hi.astype(F32)
        mid = r1.astype(BF16)
        lo = (r1 - mid.astype(F32)).astype(BF16)
        cs3 = _dot(tri_ref[...], jnp.concatenate([hi, mid, lo], axis=1))
        cs = cs3[:, 0:LANES] + cs3[:, LANES:2 * LANES] + cs3[:, 2 * LANES:3 * LANES]
        last = cs[UNIT - 1:UNIT, :]
        e_cs = jnp.exp(cs)
        w_end = dt * jnp.exp(last - cs)
        cdec = jnp.broadcast_to(jnp.exp(last), (SUBLANES, LANES))
        cs2 = cs * LOG2_E
        stacked = jnp.concatenate([w_end, e_cs, cdec], axis=0)
        s_hi, s_lo = _split2(stacked)
        expanded = _dot(jnp.concatenate([s_hi, s_lo], axis=1), e2_ref[...])
        unit_vals[k] = dict(cs2=cs2, cs2_t=cs2.T, dt_t=dt.T, w_exp=expanded[0:UNIT],
                            e_exp=expanded[UNIT:2 * UNIT],
                            cdec_exp=expanded[2 * UNIT:2 * UNIT + 1])

    piece_vals = {}

    def piece_slices(k, g):
        return (slice(k * UNIT, (k + 1) * UNIT), slice(g * SSD_STATE, (g + 1) * SSD_STATE),
                slice(g * SSD_GROUP_WIDTH, (g + 1) * SSD_GROUP_WIDTH))

    def piece_products(k, g):
        rows, scols, _ = piece_slices(k, g)
        b_g = bm_ref[0, rows, scols]
        c_g = cm_ref[0, rows, scols]
        state_t = st_ref[g]
        cb = lax.dot_general(c_g, b_g, (((1,), (1,)), ((), ())), preferred_element_type=F32)
        piece_vals[k, g] = dict(cb=cb, y_off=_dot(c_g, state_t.astype(BF16)), state_t=state_t)

    def piece_decay(k, g):
        u, p = unit_vals[k], piece_vals[k, g]
        rows, scols, gcols = piece_slices(k, g)
        x_g = xs_ref[0, rows, gcols]
        decays, x_heads = [], []
        for r in range(SSD_HEADS_PER_GROUP):
            h = g * SSD_HEADS_PER_GROUP + r
            seg2 = u["cs2"][:, h:h + 1] - u["cs2_t"][h:h + 1, :]
            decay = jnp.where(causal, jnp.exp2(seg2), 0.0) * u["dt_t"][h:h + 1, :]
            decays.append((p["cb"] * decay).astype(BF16))
            x_heads.append(jnp.where(head_lanes[r], x_g, jnp.zeros_like(x_g)))
        p["y_diag"] = _dot(jnp.concatenate(decays, axis=1), jnp.concatenate(x_heads, axis=0))
        x_dec = (x_g.astype(F32) * u["w_exp"][:, gcols]).astype(BF16)
        new = lax.dot_general(bm_ref[0, rows, scols], x_dec, (((0,), (0,)), ((), ())),
                              preferred_element_type=F32)
        st_ref[g] = u["cdec_exp"][:, gcols] * p["state_t"] + new

    def piece_output(k, g):
        u, p = unit_vals[k], piece_vals.pop((k, g))
        rows, _, gcols = piece_slices(k, g)
        y = (p["y_diag"] + p["y_off"] * u["e_exp"][:, gcols]
             + dexp_ref[:, gcols] * xs_ref[0, rows, gcols].astype(F32))
        yg = y * zs_ref[0, rows, gcols].astype(F32)
        ms = jnp.mean(yg * yg, axis=-1, keepdims=True)
        yn_ref[cur, rows, gcols] = (yg * lax.rsqrt(ms + RMS_EPS) * nw_ref[:, gcols]).astype(BF16)

    def merge_block(r0, c0):
        rows, cols = slice(r0, r0 + MERGE_ROWS), slice(c0, c0 + MERGE_COLS)
        y_ssd = _dot(yn_ref[prev, rows, :], wsp_ref[:, cols])
        merged_s[rows, cols] = (gyp_ref[0, rows, cols].astype(F32)
                                + gb_ref[0, rows, cols].astype(F32) * y_ssd).astype(BF16)

    def reorder_unit(k):
        rows = slice(k * UNIT, (k + 1) * UNIT)
        ordered_s[rows, :] = _dot(unperm_ref[...], merged_s[rows, :]).astype(BF16)

    def out_block(r0, c0):
        rows, cols = slice(r0, r0 + MERGE_ROWS), slice(c0, c0 + MERGE_COLS)
        resid_s[rows, cols] = (DEEPNORM_ALPHA * x_ref[0, rows, cols]
                               + _dot(ordered_s[rows, :], wout_ref[:, cols]))

    def norm_unit(k):
        rows = slice(k * UNIT, (k + 1) * UNIT)
        h1_ref[0, rows, :] = _layer_norm(resid_s[rows, :], g1_ref[...], b1_ref[...])

    col_blocks = range(0, D_MODEL, MERGE_COLS)
    finish = []
    for r0 in range(0, tile, MERGE_ROWS):
        units = range(r0 // UNIT, (r0 + MERGE_ROWS) // UNIT)
        finish += ([(merge_block, (r0, c0)) for c0 in col_blocks]
                   + [(reorder_unit, (k,)) for k in units]
                   + [(out_block, (r0, c0)) for c0 in col_blocks]
                   + [(norm_unit, (k,)) for k in units])
    pieces = [(k, g) for k in range(n_units) for g in range(SSD_GROUPS)]
    phases = (piece_products, piece_decay, piece_output)
    n_iters = len(pieces) + len(phases) - 1
    n_finish = len(finish)
    scan_prep(0)
    for n in range(n_iters):
        for lag, phase in enumerate(phases):
            if 0 <= n - lag < len(pieces):
                phase(*pieces[n - lag])
        if n % SSD_GROUPS == SSD_GROUPS // 2 and n // SSD_GROUPS + 1 < n_units:
            scan_prep(n // SSD_GROUPS + 1)
        while n_finish - len(finish) < ((n + 1) * n_finish) // n_iters:
            fin, args = finish.pop(0)
            fin(*args)


def _ssd_call(xs, bm, cm, dt, zs, gyp, gb, x, a_row, dexp, nw, tri, unperm, e2, wsp, wout, g1, b1):
    bsz, seq, _ = x.shape
    tile = min(SSD_TILE, seq)
    assert seq % tile == 0 and tile % UNIT == 0
    tiles_per_seq = seq // tile
    n_tiles = bsz * tiles_per_seq

    def tile_spec(width, lag):
        def index_map(t):
            i = jnp.clip(t - lag, 0, n_tiles - 1)
            return (i // tiles_per_seq, i % tiles_per_seq, 0)
        return pl.BlockSpec((1, tile, width), index_map)

    consts = (a_row, dexp, nw, tri, unperm, e2, wsp, wout, g1, b1)
    return pl.pallas_call(
        functools.partial(_ssd_kernel, tiles_per_seq),
        grid=(n_tiles + 1,),
        in_specs=[tile_spec(a.shape[-1], 0) for a in (xs, bm, cm, dt, zs)]
                 + [tile_spec(a.shape[-1], 1) for a in (gyp, gb, x)]
                 + [_const_spec(a.shape) for a in consts],
        out_specs=tile_spec(D_MODEL, 1),
        out_shape=jax.ShapeDtypeStruct((bsz, seq, D_MODEL), F32),
        scratch_shapes=[pltpu.VMEM((SSD_GROUPS, SSD_STATE, SSD_GROUP_WIDTH), F32),
                        pltpu.VMEM((2, tile, SSD_INNER), BF16),
                        pltpu.VMEM((tile, D_MODEL), BF16),
                        pltpu.VMEM((tile, D_MODEL), BF16),
                        pltpu.VMEM((tile, D_MODEL), F32)],
        compiler_params=pltpu.CompilerParams(
            dimension_semantics=("arbitrary",),
            vmem_limit_bytes=VMEM_LIMIT_BYTES),
        name="ssd_merge_ln1",
    )(xs, bm, cm, dt, zs, gyp, gb, x, *consts)


def _mlp_kernel(h_ref, wup_ref, wdn_ref, g_ref, b_ref, o_ref):
    h = h_ref[...]
    hb = h.astype(BF16)
    ff = jnp.zeros(h.shape, F32)
    for j in range(D_FF // MLP_COLS):
        cols = slice(j * MLP_COLS, (j + 1) * MLP_COLS)
        up = jnp.maximum(_dot(hb, wup_ref[:, cols]), 0.0)
        ff = ff + _dot((up * up).astype(BF16), wdn_ref[cols, :])
    o_ref[...] = _layer_norm(DEEPNORM_ALPHA * h + ff, g_ref[...], b_ref[...])


def _mlp_call(h, wup, wdn, g, b):
    rows = h.shape[0]
    tile = min(MLP_TILE, rows)
    assert rows % tile == 0
    tok = pl.BlockSpec((tile, D_MODEL), lambda i: (i, 0))
    return pl.pallas_call(
        _mlp_kernel,
        grid=(rows // tile,),
        in_specs=[tok] + [_const_spec(a.shape) for a in (wup, wdn, g, b)],
        out_specs=tok,
        out_shape=jax.ShapeDtypeStruct(h.shape, F32),
        compiler_params=pltpu.CompilerParams(
            dimension_semantics=("arbitrary",),
            vmem_limit_bytes=VMEM_LIMIT_BYTES),
        name="mlp_ln2",
    )(h, wup, wdn, g, b)


def _layer(x, w_in, b_gates, conv_w, conv_b, dt_bias, a_log, d_skip, ssd_norm_w, w_ssd_proj,
           w_pool_group, pool_scale, w_out, ln1_g, ln1_b, w_up, w_down, ln2_g, ln2_b):
    bsz, seq, _ = x.shape
    row = lambda v: v.astype(F32).reshape(1, -1)
    pad_heads = lambda v: jnp.pad(v.astype(F32), (0, LANES - SSD_HEADS)).reshape(1, LANES)

    idx = jnp.arange(UNIT)
    tok_of_row = _unit_token(idx)
    perm = (tok_of_row[:, None] == idx[None, :]).astype(BF16)
    unperm = perm.T
    tri = (tok_of_row[:, None] >= tok_of_row[None, :]).astype(BF16)
    head_of = jnp.arange(SSD_INNER) // SSD_HEAD_DIM
    e1 = (jnp.arange(LANES)[:, None] == head_of[None, :]).astype(BF16)
    e2 = jnp.concatenate([e1, e1], axis=0)

    wb = w_in.astype(BF16)
    wp, wz, wxbc = wb[:, :OFF_POOL], wb[:, OFF_POOL:OFF_Z], wb[:, OFF_Z:OFF_XBC]
    wdt = jnp.pad(wb[:, OFF_XBC:OFF_DT], ((0, 0), (0, LANES - SSD_HEADS)))
    wg = wb[:, OFF_DT:]
    xs, bm, cm, dt, zs, gyp, gb = _proj_call(
        x, perm, wp, wz, wxbc, wdt, wg, row(b_gates), conv_w.astype(F32), row(conv_b),
        pad_heads(dt_bias), w_pool_group.astype(BF16), row(pool_scale))

    a_row = pad_heads(-jnp.exp(a_log.astype(F32)))
    dexp = jnp.repeat(d_skip.astype(F32), SSD_HEAD_DIM).reshape(1, SSD_INNER)
    h1 = _ssd_call(xs, bm, cm, dt, zs, gyp, gb, x, a_row, dexp, row(ssd_norm_w), tri, unperm, e2,
                   w_ssd_proj.astype(BF16), w_out.astype(BF16), row(ln1_g), row(ln1_b))

    h2 = _mlp_call(h1.reshape(bsz * seq, D_MODEL), w_up.astype(BF16), w_down.astype(BF16),
                   row(ln2_g), row(ln2_b))
    return h2.reshape(bsz, seq, D_MODEL)


def kernel(x, w_in, b_gates, conv_w, conv_b, dt_bias, a_log, d_skip, ssd_norm_w, w_ssd_proj,
           w_pool_group, pool_scale, w_out, ln1_g, ln1_b, w_up, w_down, ln2_g, ln2_b):
    h = x
    for layer in range(w_in.shape[0]):
        h = _layer(h, w_in[layer], b_gates[layer], conv_w[layer], conv_b[layer], dt_bias[layer],
                   a_log[layer], d_skip[layer], ssd_norm_w[layer], w_ssd_proj[layer],
                   w_pool_group[layer], pool_scale[layer], w_out[layer], ln1_g[layer],
                   ln1_b[layer], w_up[layer], w_down[layer], ln2_g[layer], ln2_b[layer])
    return h
```

```python
import functools
import math

import jax
import jax.numpy as jnp
from jax import lax
from jax.experimental import pallas as pl
from jax.experimental.pallas import tpu as pltpu

F32 = jnp.float32
BF16 = jnp.bfloat16

D_MODEL = 1024
POOL_WINDOWS = (2, 4, 8, 16)
POOL_GROUP_WIDTH = D_MODEL // len(POOL_WINDOWS)
SSD_INNER = 2 * D_MODEL
SSD_HEAD_DIM = 64
SSD_HEADS = SSD_INNER // SSD_HEAD_DIM
SSD_GROUPS = 8
SSD_HEADS_PER_GROUP = SSD_HEADS // SSD_GROUPS
SSD_GROUP_WIDTH = SSD_INNER // SSD_GROUPS
SSD_STATE = 128
SSD_CONV = 4
SSD_BC_WIDTH = SSD_GROUPS * SSD_STATE
SSD_CONV_DIM = SSD_INNER + 2 * SSD_BC_WIDTH
ACT_WIDTHS = (SSD_INNER, SSD_BC_WIDTH, SSD_BC_WIDTH, SSD_INNER)
D_FF = 4 * D_MODEL
OFF_POOL = D_MODEL
OFF_Z = OFF_POOL + SSD_INNER
OFF_XBC = OFF_Z + SSD_CONV_DIM
OFF_DT = OFF_XBC + SSD_HEADS
DEEPNORM_ALPHA = 2.0 ** 0.25
LN_EPS = 1e-5
RMS_EPS = 1e-5
LOG2_E = math.log2(math.e)

LANES = 128
SUBLANES = 8
VMEM_LIMIT_BYTES = 56 * 1024 * 1024
UNIT = 128
UNIT_Q = UNIT // SUBLANES
POOL_HALO = max(POOL_WINDOWS) - 1
CONV_HALO = SSD_CONV - 1
PROJ_TILE = 512
PROJ_COLS = 1024
PARK_SLOTS = 3
SSD_TILE = 512
MERGE_COLS = 256
MERGE_ROWS = 256
MLP_TILE = 1024
MLP_COLS = 1024


def _dot(a, b):
    return jnp.dot(a, b, preferred_element_type=F32)


def _const_spec(shape):
    return pl.BlockSpec(shape, lambda *_: (0,) * len(shape), pipeline_mode=pl.Buffered(1))


def _layer_norm(v, g, b):
    mu = jnp.mean(v, axis=-1, keepdims=True)
    d = v - mu
    var = jnp.mean(d * d, axis=-1, keepdims=True)
    return d * lax.rsqrt(var + LN_EPS) * g + b


def _sigmoid(v):
    return 0.5 * jnp.tanh(0.5 * v) + 0.5


def _silu(v):
    h = 0.5 * v
    return h + h * jnp.tanh(h)


def _split2(v):
    hi = v.astype(BF16)
    lo = (v - hi.astype(F32)).astype(BF16)
    return hi, lo


def _unit_token(rows):
    return (rows & (SUBLANES - 1)) * UNIT_Q + (rows >> (SUBLANES.bit_length() - 1))


def _halo(prev_tail, cur_tail):
    n = cur_tail.shape[0] // SUBLANES
    last_sublane = lax.broadcasted_iota(jnp.int32, (SUBLANES, cur_tail.shape[1]), 0) == SUBLANES - 1
    blocks = []
    for j in range(n):
        rows = slice(j * SUBLANES, (j + 1) * SUBLANES)
        blocks.append(pltpu.roll(jnp.where(last_sublane, prev_tail[rows], cur_tail[rows]), 1, 0))
    return blocks[0] if n == 1 else jnp.concatenate(blocks, axis=0)


def _lagged(halo, unit, lag):
    if lag == 0:
        return unit
    hrows = halo.shape[0]
    return jnp.concatenate([halo[hrows - SUBLANES * lag:hrows], unit[:UNIT - SUBLANES * lag]], axis=0)


def _col_views(ref, widths):
    views, start = [], 0
    for width in widths:
        views.append(ref.at[:, :, start:start + width])
        start += width
    assert start == ref.shape[2]
    return views


def _act_views(act_ref):
    return _col_views(act_ref, ACT_WIDTHS)


def _gate_views(gate_ref):
    return _col_views(gate_ref, (D_MODEL, D_MODEL))


def _proj_kernel(x_ref, perm_ref, wp_ref, wz_ref, wxbc_ref, wdt_ref, wg_ref, bg_ref,
                 cw_ref, cb_ref, dtb_ref, wpg_ref, ps_ref,
                 act_ref, dt_ref, gate_ref,
                 utail, ctail, *parked):
    xs_ref, bm_ref, cm_ref, zs_ref = _act_views(act_ref)
    gyp_ref, gb_ref = _gate_views(gate_ref)
    tile = x_ref.shape[1]
    n_units = tile // UNIT
    step = pl.program_id(1)
    pool_tail = SUBLANES * POOL_HALO
    conv_tail = SUBLANES * CONV_HALO

    @pl.when(step == 0)
    def _():
        utail[...] = jnp.zeros(utail.shape, F32)
        ctail[...] = jnp.zeros(ctail.shape, F32)

    x_nat = x_ref[0].astype(BF16)
    xb = jnp.concatenate(
        [_dot(perm_ref[...], x_nat[k * UNIT:(k + 1) * UNIT]).astype(BF16) for k in range(n_units)],
        axis=0)

    rows = lax.broadcasted_iota(jnp.int32, (UNIT, 1), 0)
    unit_pos = (_unit_token(rows) + 1).astype(F32)
    y_pool = {}

    def pool_epilogue(c0, u):
        outs = []
        for gi in range(c0 // POOL_GROUP_WIDTH, (c0 + u.shape[1]) // POOL_GROUP_WIDTH):
            win = POOL_WINDOWS[gi]
            cols = slice(gi * POOL_GROUP_WIDTH, (gi + 1) * POOL_GROUP_WIDTH)
            local = slice(cols.start - c0, cols.stop - c0)
            hrows = SUBLANES * (win - 1)
            pooled = []
            for k in range(n_units):
                ug = u[k * UNIT:(k + 1) * UNIT, local]
                cur_tail = ug[UNIT - hrows:]
                if k == 0:
                    prev_tail = utail[pool_tail - hrows:pool_tail, cols]
                else:
                    prev_tail = u[k * UNIT - hrows:k * UNIT, local]
                halo = _halo(prev_tail, cur_tail)
                acc = ug
                for lag in range(1, win):
                    acc = acc + _lagged(halo, ug, lag)
                pos = unit_pos + (step * tile + k * UNIT).astype(F32)
                inv_cnt = 1.0 / jnp.minimum(pos, float(win))
                pooled.append((acc * inv_cnt - ug).astype(BF16))
            pooled = jnp.concatenate(pooled, axis=0)
            outs.append(_dot(pooled, wpg_ref[gi]) * ps_ref[:, cols])
        utail[:, c0:c0 + u.shape[1]] = u[tile - pool_tail:, :]
        y_pool[c0] = jnp.concatenate(outs, axis=1)

    def conv_epilogue(c0, raw):
        cols = slice(c0, c0 + raw.shape[1])
        half_w = 0.5 * cw_ref[:, cols]
        half_b = 0.5 * cb_ref[:, cols]
        acts = []
        for k in range(n_units):
            unit = raw[k * UNIT:(k + 1) * UNIT]
            cur_tail = unit[UNIT - conv_tail:]
            prev_tail = ctail[:, cols] if k == 0 else raw[k * UNIT - conv_tail:k * UNIT]
            halo = _halo(prev_tail, cur_tail)
            half = unit * half_w[SSD_CONV - 1:SSD_CONV] + half_b
            for tap in range(SSD_CONV - 1):
                half = half + _lagged(halo, unit, SSD_CONV - 1 - tap) * half_w[tap:tap + 1]
            acts.append((half + half * jnp.tanh(half)).astype(BF16))
        ctail[:, cols] = raw[tile - conv_tail:]
        act = jnp.concatenate(acts, axis=0)
        for ref, start in ((xs_ref, 0), (bm_ref, SSD_INNER), (cm_ref, SSD_INNER + SSD_BC_WIDTH)):
            if start <= c0 < start + ref.shape[2]:
                ref[0, :, c0 - start:c0 - start + raw.shape[1]] = act

    def gate_epilogue(c0, raw):
        gate = _sigmoid(raw + bg_ref[:, c0:c0 + raw.shape[1]])
        if c0 < D_MODEL:
            gyp_ref[0, :, c0:c0 + raw.shape[1]] = (gate * y_pool[c0]).astype(BF16)
        else:
            gb_ref[0, :, c0 - D_MODEL:c0 - D_MODEL + raw.shape[1]] = gate.astype(BF16)

    def z_epilogue(c0, raw):
        zs_ref[0, :, c0:c0 + raw.shape[1]] = _silu(raw).astype(BF16)

    def dt_epilogue(c0, raw):
        dt_ref[0] = jax.nn.softplus(raw + dtb_ref[...])

    blocks = lambda ref, fn: [(ref, c0, PROJ_COLS, fn) for c0 in range(0, ref.shape[1], PROJ_COLS)]
    heavy = blocks(wxbc_ref, conv_epilogue)
    light = blocks(wg_ref, gate_epilogue) + blocks(wz_ref, z_epilogue)
    assert len(heavy) == len(light)
    stages = blocks(wp_ref, pool_epilogue)
    for pair in zip(heavy, light):
        stages += pair
    stages.append((wdt_ref, 0, LANES, dt_epilogue))

    runtime_zero = jnp.minimum(step, 0)

    def issue(i):
        ref, c0, width, _ = stages[i]
        parked[i % len(parked)][runtime_zero, :, 0:width] = _dot(xb, ref[:, c0:c0 + width])

    issue(0)
    for i, (_, c0, width, epilogue) in enumerate(stages):
        if i + 1 < len(stages):
            issue(i + 1)
        epilogue(c0, parked[i % len(parked)][runtime_zero, :, 0:width])


def _proj_call(x, perm, wp, wz, wxbc, wdt, wg, bg, cw, cb, dtb, wpg, ps):
    bsz, seq, _ = x.shape
    tile = min(PROJ_TILE, seq)
    assert seq % tile == 0 and tile % UNIT == 0 and POOL_HALO < UNIT_Q
    assert SSD_BC_WIDTH % PROJ_COLS == 0 and D_MODEL % PROJ_COLS == 0
    tok = lambda w: pl.BlockSpec((1, tile, w), lambda b, s: (b, s, 0))
    out_widths = (sum(ACT_WIDTHS), LANES, 2 * D_MODEL)
    out_dtypes = (BF16, F32, BF16)
    consts = (perm, wp, wz, wxbc, wdt, wg, bg, cw, cb, dtb, wpg, ps)
    return pl.pallas_call(
        _proj_kernel,
        grid=(bsz, seq // tile),
        in_specs=[tok(D_MODEL)] + [_const_spec(a.shape) for a in consts],
        out_specs=[tok(w) for w in out_widths],
        out_shape=[jax.ShapeDtypeStruct((bsz, seq, w), d) for w, d in zip(out_widths, out_dtypes)],
        scratch_shapes=[pltpu.VMEM((SUBLANES * POOL_HALO, D_MODEL), F32),
                        pltpu.VMEM((SUBLANES * CONV_HALO, SSD_CONV_DIM), F32)]
                       + [pltpu.VMEM((1, tile, PROJ_COLS), F32) for _ in range(PARK_SLOTS)],
        compiler_params=pltpu.CompilerParams(
            dimension_semantics=("arbitrary", "arbitrary"),
            vmem_limit_bytes=VMEM_LIMIT_BYTES),
        name="proj_pool_conv",
    )(x, *consts)


def _ssd_kernel(tiles_per_seq,
                act_ref, dt_ref, gate_ref, x_ref,
                a_ref, dexp_ref, nw_ref, tri_ref, unperm_ref, e2_ref, wsp_ref, wout_ref,
                g1_ref, b1_ref, h1_ref, st_ref, yn_ref, merged_s, ordered_s, resid_s):
    xs_ref, bm_ref, cm_ref, zs_ref = _act_views(act_ref)
    gyp_ref, gb_ref = _gate_views(gate_ref)
    tile = xs_ref.shape[1]
    n_units = tile // UNIT
    t = pl.program_id(0)
    cur = t % 2
    prev = 1 - cur

    @pl.when(t % tiles_per_seq == 0)
    def _():
        st_ref[...] = jnp.zeros(st_ref.shape, F32)

    @pl.when(t == 0)
    def _():
        yn_ref[...] = jnp.zeros(yn_ref.shape, BF16)

    tok_row = _unit_token(lax.broadcasted_iota(jnp.int32, (UNIT, UNIT), 0))
    tok_col = _unit_token(lax.broadcasted_iota(jnp.int32, (UNIT, UNIT), 1))
    causal = tok_row >= tok_col
    lane = lax.broadcasted_iota(jnp.int32, (UNIT, SSD_GROUP_WIDTH), 1)
    head_lanes = [(lane >= r * SSD_HEAD_DIM) & (lane < (r + 1) * SSD_HEAD_DIM)
                  for r in range(SSD_HEADS_PER_GROUP)]
    unit_vals = {}

    def scan_prep(k):
        rows = slice(k * UNIT, (k + 1) * UNIT)
        dt = dt_ref[0, rows, :]
        da = dt * a_ref[...]
        hi = da.astype(BF16)
        r1 = da - hi.astype(F32)
        mid = r1.astype(BF16)
        lo = (r1 - mid.astype(F32)).astype(BF16)
        cs3 = _dot(tri_ref[...], jnp.concatenate([hi, mid, lo], axis=1))
        cs = cs3[:, 0:LANES] + cs3[:, LANES:2 * LANES] + cs3[:, 2 * LANES:3 * LANES]
        last = cs[UNIT - 1:UNIT, :]
        e_cs = jnp.exp(cs)
        w_end = dt * jnp.exp(last - cs)
        cs2 = cs * LOG2_E
        stacked = jnp.concatenate([w_end, e_cs], axis=0)
        s_hi, s_lo = _split2(stacked)
        expanded = _dot(jnp.concatenate([s_hi, s_lo], axis=1), e2_ref[...])
        unit_vals[k] = dict(cs2=cs2, cs2_t=cs2.T, dt_t=dt.T, w_exp=expanded[0:UNIT],
                            e_exp=expanded[UNIT:2 * UNIT],
                            cdec_exp=expanded[2 * UNIT - 1:2 * UNIT])

    piece_vals = {}

    def piece_slices(k, g):
        return (slice(k * UNIT, (k + 1) * UNIT), slice(g * SSD_STATE, (g + 1) * SSD_STATE),
                slice(g * SSD_GROUP_WIDTH, (g + 1) * SSD_GROUP_WIDTH))

    def piece_products(k, g):
        rows, scols, _ = piece_slices(k, g)
        b_g = bm_ref[0, rows, scols]
        c_g = cm_ref[0, rows, scols]
        state_t = st_ref[g]
        cb = lax.dot_general(c_g, b_g, (((1,), (1,)), ((), ())), preferred_element_type=F32)
        piece_vals[k, g] = dict(cb=cb, y_off=_dot(c_g, state_t.astype(BF16)))

    def piece_decay(k, g):
        u, p = unit_vals[k], piece_vals[k, g]
        rows, scols, gcols = piece_slices(k, g)
        x_g = xs_ref[0, rows, gcols]
        decays, x_heads = [], []
        for r in range(SSD_HEADS_PER_GROUP):
            h = g * SSD_HEADS_PER_GROUP + r
            seg2 = u["cs2"][:, h:h + 1] - u["cs2_t"][h:h + 1, :]
            decay = jnp.where(causal, jnp.exp2(seg2), 0.0) * u["dt_t"][h:h + 1, :]
            decays.append((p["cb"] * decay).astype(BF16))
            x_heads.append(jnp.where(head_lanes[r], x_g, jnp.zeros_like(x_g)))
        p["y_diag"] = _dot(jnp.concatenate(decays, axis=1), jnp.concatenate(x_heads, axis=0))
        x_dec = (x_g.astype(F32) * u["w_exp"][:, gcols]).astype(BF16)
        new = lax.dot_general(bm_ref[0, rows, scols], x_dec, (((0,), (0,)), ((), ())),
                              preferred_element_type=F32)
        st_ref[g] = u["cdec_exp"][:, gcols] * st_ref[g] + new

    def piece_output(k, g):
        u, p = unit_vals[k], piece_vals.pop((k, g))
        rows, _, gcols = piece_slices(k, g)
        y = (p["y_diag"] + p["y_off"] * u["e_exp"][:, gcols]
             + dexp_ref[:, gcols] * xs_ref[0, rows, gcols].astype(F32))
        yg = y * zs_ref[0, rows, gcols].astype(F32)
        ms = jnp.mean(yg * yg, axis=-1, keepdims=True)
        yn_ref[cur, rows, gcols] = (yg * lax.rsqrt(ms + RMS_EPS) * nw_ref[:, gcols]).astype(BF16)

    def merge_block(r0, c0):
        rows, cols = slice(r0, r0 + MERGE_ROWS), slice(c0, c0 + MERGE_COLS)
        y_ssd = _dot(yn_ref[prev, rows, :], wsp_ref[:, cols])
        merged_s[rows, cols] = (gyp_ref[0, rows, cols].astype(F32)
                                + gb_ref[0, rows, cols].astype(F32) * y_ssd).astype(BF16)

    def reorder_unit(k):
        rows = slice(k * UNIT, (k + 1) * UNIT)
        ordered_s[rows, :] = _dot(unperm_ref[...], merged_s[rows, :]).astype(BF16)

    def out_block(r0, c0):
        rows, cols = slice(r0, r0 + MERGE_ROWS), slice(c0, c0 + MERGE_COLS)
        resid_s[rows, cols] = (DEEPNORM_ALPHA * x_ref[0, rows, cols]
                               + _dot(ordered_s[rows, :], wout_ref[:, cols]))

    def norm_unit(k):
        rows = slice(k * UNIT, (k + 1) * UNIT)
        h1_ref[0, rows, :] = _layer_norm(resid_s[rows, :], g1_ref[...], b1_ref[...])

    col_blocks = range(0, D_MODEL, MERGE_COLS)
    finish = []
    for r0 in range(0, tile, MERGE_ROWS):
        units = range(r0 // UNIT, (r0 + MERGE_ROWS) // UNIT)
        finish += ([(merge_block, (r0, c0)) for c0 in col_blocks]
                   + [(reorder_unit, (k,)) for k in units]
                   + [(out_block, (r0, c0)) for c0 in col_blocks]
                   + [(norm_unit, (k,)) for k in units])
    pieces = [(k, g) for k in range(n_units) for g in range(SSD_GROUPS)]
    phases = (piece_products, piece_decay, piece_output)
    n_iters = len(pieces) + len(phases) - 1
    n_finish = len(finish)
    scan_prep(0)
    for n in range(n_iters):
        for lag, phase in enumerate(phases):
            if 0 <= n - lag < len(pieces):
                phase(*pieces[n - lag])
        if n % SSD_GROUPS == SSD_GROUPS // 2 and n // SSD_GROUPS + 1 < n_units:
            scan_prep(n // SSD_GROUPS + 1)
        while n_finish - len(finish) < ((n + 1) * n_finish) // n_iters:
            fin, args = finish.pop(0)
            fin(*args)


def _ssd_call(act, dt, gates, x, a_row, dexp, nw, tri, unperm, e2, wsp, wout, g1, b1):
    bsz, seq, _ = x.shape
    tile = min(SSD_TILE, seq)
    assert seq % tile == 0 and tile % UNIT == 0
    tiles_per_seq = seq // tile
    n_tiles = bsz * tiles_per_seq

    def tile_spec(width, lag):
        def index_map(t):
            i = jnp.clip(t - lag, 0, n_tiles - 1)
            return (i // tiles_per_seq, i % tiles_per_seq, 0)
        return pl.BlockSpec((1, tile, width), index_map)

    consts = (a_row, dexp, nw, tri, unperm, e2, wsp, wout, g1, b1)
    return pl.pallas_call(
        functools.partial(_ssd_kernel, tiles_per_seq),
        grid=(n_tiles + 1,),
        in_specs=[tile_spec(a.shape[-1], 0) for a in (act, dt)]
                 + [tile_spec(a.shape[-1], 1) for a in (gates, x)]
                 + [_const_spec(a.shape) for a in consts],
        out_specs=tile_spec(D_MODEL, 1),
        out_shape=jax.ShapeDtypeStruct((bsz, seq, D_MODEL), F32),
        scratch_shapes=[pltpu.VMEM((SSD_GROUPS, SSD_STATE, SSD_GROUP_WIDTH), F32),
                        pltpu.VMEM((2, tile, SSD_INNER), BF16),
                        pltpu.VMEM((tile, D_MODEL), BF16),
                        pltpu.VMEM((tile, D_MODEL), BF16),
                        pltpu.VMEM((tile, D_MODEL), F32)],
        compiler_params=pltpu.CompilerParams(
            dimension_semantics=("arbitrary",),
            vmem_limit_bytes=VMEM_LIMIT_BYTES),
        name="ssd_merge_ln1",
    )(act, dt, gates, x, *consts)


def _mlp_kernel(h_ref, wup_ref, wdn_ref, g_ref, b_ref, o_ref):
    h = h_ref[...]
    hb = h.astype(BF16)
    ff = jnp.zeros(h.shape, F32)
    for j in range(D_FF // MLP_COLS):
        cols = slice(j * MLP_COLS, (j + 1) * MLP_COLS)
        up = jnp.maximum(_dot(hb, wup_ref[:, cols]), 0.0)
        ff = ff + _dot((up * up).astype(BF16), wdn_ref[cols, :])
    o_ref[...] = _layer_norm(DEEPNORM_ALPHA * h + ff, g_ref[...], b_ref[...])


def _mlp_call(h, wup, wdn, g, b):
    rows = h.shape[0]
    tile = min(MLP_TILE, rows)
    assert rows % tile == 0
    tok = pl.BlockSpec((tile, D_MODEL), lambda i: (i, 0))
    return pl.pallas_call(
        _mlp_kernel,
        grid=(rows // tile,),
        in_specs=[tok] + [_const_spec(a.shape) for a in (wup, wdn, g, b)],
        out_specs=tok,
        out_shape=jax.ShapeDtypeStruct(h.shape, F32),
        compiler_params=pltpu.CompilerParams(
            dimension_semantics=("arbitrary",),
            vmem_limit_bytes=VMEM_LIMIT_BYTES),
        name="mlp_ln2",
    )(h, wup, wdn, g, b)


def _layer(x, w_in, b_gates, conv_w, conv_b, dt_bias, a_log, d_skip, ssd_norm_w, w_ssd_proj,
           w_pool_group, pool_scale, w_out, ln1_g, ln1_b, w_up, w_down, ln2_g, ln2_b):
    bsz, seq, _ = x.shape
    row = lambda v: v.astype(F32).reshape(1, -1)
    pad_heads = lambda v: jnp.pad(v.astype(F32), (0, LANES - SSD_HEADS)).reshape(1, LANES)

    idx = jnp.arange(UNIT)
    tok_of_row = _unit_token(idx)
    perm = (tok_of_row[:, None] == idx[None, :]).astype(BF16)
    unperm = perm.T
    tri = (tok_of_row[:, None] >= tok_of_row[None, :]).astype(BF16)
    head_of = jnp.arange(SSD_INNER) // SSD_HEAD_DIM
    e1 = (jnp.arange(LANES)[:, None] == head_of[None, :]).astype(BF16)
    e2 = jnp.concatenate([e1, e1], axis=0)

    wb = w_in.astype(BF16)
    wp, wz, wxbc = wb[:, :OFF_POOL], wb[:, OFF_POOL:OFF_Z], wb[:, OFF_Z:OFF_XBC]
    wdt = jnp.pad(wb[:, OFF_XBC:OFF_DT], ((0, 0), (0, LANES - SSD_HEADS)))
    wg = wb[:, OFF_DT:]
    act, dt, gates = _proj_call(
        x, perm, wp, wz, wxbc, wdt, wg, row(b_gates), conv_w.astype(F32), row(conv_b),
        pad_heads(dt_bias), w_pool_group.astype(BF16), row(pool_scale))

    a_row = pad_heads(-jnp.exp(a_log.astype(F32)))
    dexp = jnp.repeat(d_skip.astype(F32), SSD_HEAD_DIM).reshape(1, SSD_INNER)
    h1 = _ssd_call(act, dt, gates, x, a_row, dexp, row(ssd_norm_w), tri, unperm, e2,
                   w_ssd_proj.astype(BF16), w_out.astype(BF16), row(ln1_g), row(ln1_b))

    h2 = _mlp_call(h1.reshape(bsz * seq, D_MODEL), w_up.astype(BF16), w_down.astype(BF16),
                   row(ln2_g), row(ln2_b))
    return h2.reshape(bsz, seq, D_MODEL)


def kernel(x, w_in, b_gates, conv_w, conv_b, dt_bias, a_log, d_skip, ssd_norm_w, w_ssd_proj,
           w_pool_group, pool_scale, w_out, ln1_g, ln1_b, w_up, w_down, ln2_g, ln2_b):
    h = x
    for layer in range(w_in.shape[0]):
        h = _layer(h, w_in[layer], b_gates[layer], conv_w[layer], conv_b[layer], dt_bias[layer],
                   a_log[layer], d_skip[layer], ssd_norm_w[layer], w_ssd_proj[layer],
                   w_pool_group[layer], pool_scale[layer], w_out[layer], ln1_g[layer],
                   ln1_b[layer], w_up[layer], w_down[layer], ln2_g[layer], ln2_b[layer])
    return h
```

```python
import functools
import math

import jax
import jax.numpy as jnp
from jax import lax
from jax.experimental import pallas as pl
from jax.experimental.pallas import tpu as pltpu

F32 = jnp.float32
BF16 = jnp.bfloat16

D_MODEL = 1024
POOL_WINDOWS = (2, 4, 8, 16)
POOL_GROUP_WIDTH = D_MODEL // len(POOL_WINDOWS)
SSD_INNER = 2 * D_MODEL
SSD_HEAD_DIM = 64
SSD_HEADS = SSD_INNER // SSD_HEAD_DIM
SSD_GROUPS = 8
SSD_HEADS_PER_GROUP = SSD_HEADS // SSD_GROUPS
SSD_GROUP_WIDTH = SSD_INNER // SSD_GROUPS
SSD_STATE = 128
SSD_CONV = 4
SSD_BC_WIDTH = SSD_GROUPS * SSD_STATE
SSD_CONV_DIM = SSD_INNER + 2 * SSD_BC_WIDTH
ACT_WIDTHS = (SSD_INNER, SSD_BC_WIDTH, SSD_BC_WIDTH, SSD_INNER)
D_FF = 4 * D_MODEL
OFF_POOL = D_MODEL
OFF_Z = OFF_POOL + SSD_INNER
OFF_XBC = OFF_Z + SSD_CONV_DIM
OFF_DT = OFF_XBC + SSD_HEADS
DEEPNORM_ALPHA = 2.0 ** 0.25
LN_EPS = 1e-5
RMS_EPS = 1e-5
LOG2_E = math.log2(math.e)

LANES = 128
SUBLANES = 8
VMEM_LIMIT_BYTES = 56 * 1024 * 1024
UNIT = 128
UNIT_Q = UNIT // SUBLANES
POOL_HALO = max(POOL_WINDOWS) - 1
CONV_HALO = SSD_CONV - 1
PROJ_TILE = 512
PROJ_COLS = 1024
PARK_SLOTS = 3
SSD_TILE = 512
MERGE_COLS = 256
MERGE_ROWS = 256
MLP_TILE = 1024
MLP_COLS = 1024


def _dot(a, b):
    return jnp.dot(a, b, preferred_element_type=F32)


def _const_spec(shape):
    return pl.BlockSpec(shape, lambda *_: (0,) * len(shape), pipeline_mode=pl.Buffered(1))


def _layer_norm(v, g, b):
    mu = jnp.mean(v, axis=-1, keepdims=True)
    d = v - mu
    var = jnp.mean(d * d, axis=-1, keepdims=True)
    return d * lax.rsqrt(var + LN_EPS) * g + b


def _sigmoid(v):
    return 0.5 * jnp.tanh(0.5 * v) + 0.5


def _silu(v):
    h = 0.5 * v
    return h + h * jnp.tanh(h)


def _split2(v):
    hi = v.astype(BF16)
    lo = (v - hi.astype(F32)).astype(BF16)
    return hi, lo


def _unit_token(rows):
    return (rows & (SUBLANES - 1)) * UNIT_Q + (rows >> (SUBLANES.bit_length() - 1))


def _halo(prev_tail, cur_tail):
    n = cur_tail.shape[0] // SUBLANES
    last_sublane = lax.broadcasted_iota(jnp.int32, (SUBLANES, cur_tail.shape[1]), 0) == SUBLANES - 1
    blocks = []
    for j in range(n):
        rows = slice(j * SUBLANES, (j + 1) * SUBLANES)
        blocks.append(pltpu.roll(jnp.where(last_sublane, prev_tail[rows], cur_tail[rows]), 1, 0))
    return blocks[0] if n == 1 else jnp.concatenate(blocks, axis=0)


def _lagged(halo, unit, lag):
    if lag == 0:
        return unit
    hrows = halo.shape[0]
    return jnp.concatenate([halo[hrows - SUBLANES * lag:hrows], unit[:UNIT - SUBLANES * lag]], axis=0)


def _col_views(ref, widths):
    views, start = [], 0
    for width in widths:
        views.append(ref.at[:, :, start:start + width])
        start += width
    assert start == ref.shape[2]
    return views


def _act_views(act_ref):
    return _col_views(act_ref, ACT_WIDTHS)


def _gate_views(gate_ref):
    return _col_views(gate_ref, (D_MODEL, D_MODEL))


def _proj_kernel(x_ref, perm_ref, wp_ref, wz_ref, wxbc_ref, wdt_ref, wg_ref, bg_ref,
                 cw_ref, cb_ref, dtb_ref, wpg_ref, ps_ref,
                 act_ref, dt_ref, gate_ref,
                 utail, ctail, *parked):
    xs_ref, bm_ref, cm_ref, zs_ref = _act_views(act_ref)
    gyp_ref, gb_ref = _gate_views(gate_ref)
    tile = x_ref.shape[1]
    n_units = tile // UNIT
    step = pl.program_id(1)
    pool_tail = SUBLANES * POOL_HALO
    conv_tail = SUBLANES * CONV_HALO

    @pl.when(step == 0)
    def _():
        utail[...] = jnp.zeros(utail.shape, F32)
        ctail[...] = jnp.zeros(ctail.shape, F32)

    x_nat = x_ref[0].astype(BF16)
    xb = jnp.concatenate(
        [_dot(perm_ref[...], x_nat[k * UNIT:(k + 1) * UNIT]).astype(BF16) for k in range(n_units)],
        axis=0)

    rows = lax.broadcasted_iota(jnp.int32, (UNIT, 1), 0)
    unit_pos = (_unit_token(rows) + 1).astype(F32)
    y_pool = {}

    def pool_epilogue(c0, u):
        outs = []
        for gi in range(c0 // POOL_GROUP_WIDTH, (c0 + u.shape[1]) // POOL_GROUP_WIDTH):
            win = POOL_WINDOWS[gi]
            cols = slice(gi * POOL_GROUP_WIDTH, (gi + 1) * POOL_GROUP_WIDTH)
            local = slice(cols.start - c0, cols.stop - c0)
            hrows = SUBLANES * (win - 1)
            pooled = []
            for k in range(n_units):
                ug = u[k * UNIT:(k + 1) * UNIT, local]
                cur_tail = ug[UNIT - hrows:]
                if k == 0:
                    prev_tail = utail[pool_tail - hrows:pool_tail, cols]
                else:
                    prev_tail = u[k * UNIT - hrows:k * UNIT, local]
                halo = _halo(prev_tail, cur_tail)
                acc = ug
                for lag in range(1, win):
                    acc = acc + _lagged(halo, ug, lag)
                pos = unit_pos + (step * tile + k * UNIT).astype(F32)
                inv_cnt = 1.0 / jnp.minimum(pos, float(win))
                pooled.append((acc * inv_cnt - ug).astype(BF16))
            pooled = jnp.concatenate(pooled, axis=0)
            outs.append(_dot(pooled, wpg_ref[gi]) * ps_ref[:, cols])
        utail[:, c0:c0 + u.shape[1]] = u[tile - pool_tail:, :]
        y_pool[c0] = jnp.concatenate(outs, axis=1)

    def conv_epilogue(c0, raw):
        cols = slice(c0, c0 + raw.shape[1])
        half_w = 0.5 * cw_ref[:, cols]
        half_b = 0.5 * cb_ref[:, cols]
        acts = []
        for k in range(n_units):
            unit = raw[k * UNIT:(k + 1) * UNIT]
            cur_tail = unit[UNIT - conv_tail:]
            prev_tail = ctail[:, cols] if k == 0 else raw[k * UNIT - conv_tail:k * UNIT]
            halo = _halo(prev_tail, cur_tail)
            half = unit * half_w[SSD_CONV - 1:SSD_CONV] + half_b
            for tap in range(SSD_CONV - 1):
                half = half + _lagged(halo, unit, SSD_CONV - 1 - tap) * half_w[tap:tap + 1]
            acts.append((half + half * jnp.tanh(half)).astype(BF16))
        ctail[:, cols] = raw[tile - conv_tail:]
        act = jnp.concatenate(acts, axis=0)
        for ref, start in ((xs_ref, 0), (bm_ref, SSD_INNER), (cm_ref, SSD_INNER + SSD_BC_WIDTH)):
            if start <= c0 < start + ref.shape[2]:
                ref[0, :, c0 - start:c0 - start + raw.shape[1]] = act

    def gate_epilogue(c0, raw):
        gate = _sigmoid(raw + bg_ref[:, c0:c0 + raw.shape[1]])
        if c0 < D_MODEL:
            gyp_ref[0, :, c0:c0 + raw.shape[1]] = (gate * y_pool[c0]).astype(BF16)
        else:
            gb_ref[0, :, c0 - D_MODEL:c0 - D_MODEL + raw.shape[1]] = gate.astype(BF16)

    def z_epilogue(c0, raw):
        zs_ref[0, :, c0:c0 + raw.shape[1]] = _silu(raw).astype(BF16)

    def dt_epilogue(c0, raw):
        dt_ref[0] = jax.nn.softplus(raw + dtb_ref[...])

    blocks = lambda ref, fn: [(ref, c0, PROJ_COLS, fn) for c0 in range(0, ref.shape[1], PROJ_COLS)]
    heavy = blocks(wxbc_ref, conv_epilogue)
    light = blocks(wg_ref, gate_epilogue) + blocks(wz_ref, z_epilogue)
    assert len(heavy) == len(light)
    stages = blocks(wp_ref, pool_epilogue)
    for pair in zip(heavy, light):
        stages += pair
    stages.append((wdt_ref, 0, LANES, dt_epilogue))

    runtime_zero = jnp.minimum(step, 0)

    def issue(i):
        ref, c0, width, _ = stages[i]
        parked[i % len(parked)][runtime_zero, :, 0:width] = _dot(xb, ref[:, c0:c0 + width])

    issue(0)
    for i, (_, c0, width, epilogue) in enumerate(stages):
        if i + 1 < len(stages):
            issue(i + 1)
        epilogue(c0, parked[i % len(parked)][runtime_zero, :, 0:width])


def _proj_call(x, perm, wp, wz, wxbc, wdt, wg, bg, cw, cb, dtb, wpg, ps):
    bsz, seq, _ = x.shape
    tile = min(PROJ_TILE, seq)
    assert seq % tile == 0 and tile % UNIT == 0 and POOL_HALO < UNIT_Q
    assert SSD_BC_WIDTH % PROJ_COLS == 0 and D_MODEL % PROJ_COLS == 0
    tok = lambda w: pl.BlockSpec((1, tile, w), lambda b, s: (b, s, 0))
    out_widths = (sum(ACT_WIDTHS), LANES, 2 * D_MODEL)
    out_dtypes = (BF16, F32, BF16)
    consts = (perm, wp, wz, wxbc, wdt, wg, bg, cw, cb, dtb, wpg, ps)
    return pl.pallas_call(
        _proj_kernel,
        grid=(bsz, seq // tile),
        in_specs=[tok(D_MODEL)] + [_const_spec(a.shape) for a in consts],
        out_specs=[tok(w) for w in out_widths],
        out_shape=[jax.ShapeDtypeStruct((bsz, seq, w), d) for w, d in zip(out_widths, out_dtypes)],
        scratch_shapes=[pltpu.VMEM((SUBLANES * POOL_HALO, D_MODEL), F32),
                        pltpu.VMEM((SUBLANES * CONV_HALO, SSD_CONV_DIM), F32)]
                       + [pltpu.VMEM((1, tile, PROJ_COLS), F32) for _ in range(PARK_SLOTS)],
        compiler_params=pltpu.CompilerParams(
            dimension_semantics=("arbitrary", "arbitrary"),
            vmem_limit_bytes=VMEM_LIMIT_BYTES),
        name="proj_pool_conv",
    )(x, *consts)


def _ssd_kernel(tiles_per_seq,
                act_ref, dt_ref, gate_ref, x_ref,
                a_ref, dexp_ref, nw_ref, tri_ref, unperm_ref, e2_ref, wsp_ref, wout_ref,
                g1_ref, b1_ref, h1_ref, st_ref, yn_ref, merged_s, ordered_s, resid_s):
    xs_ref, bm_ref, cm_ref, zs_ref = _act_views(act_ref)
    gyp_ref, gb_ref = _gate_views(gate_ref)
    tile = xs_ref.shape[1]
    n_units = tile // UNIT
    t = pl.program_id(0)
    cur = t % 2
    prev = 1 - cur

    @pl.when(t % tiles_per_seq == 0)
    def _():
        st_ref[...] = jnp.zeros(st_ref.shape, F32)

    @pl.when(t == 0)
    def _():
        yn_ref[...] = jnp.zeros(yn_ref.shape, BF16)

    tok_row = _unit_token(lax.broadcasted_iota(jnp.int32, (UNIT, UNIT), 0))
    tok_col = _unit_token(lax.broadcasted_iota(jnp.int32, (UNIT, UNIT), 1))
    causal = tok_row >= tok_col
    lane = lax.broadcasted_iota(jnp.int32, (UNIT, SSD_GROUP_WIDTH), 1)
    head_lanes = [(lane >= r * SSD_HEAD_DIM) & (lane < (r + 1) * SSD_HEAD_DIM)
                  for r in range(SSD_HEADS_PER_GROUP)]
    unit_vals = {}

    def scan_prep(k):
        rows = slice(k * UNIT, (k + 1) * UNIT)
        dt = dt_ref[0, rows, :]
        da = dt * a_ref[...]
        hi = da.astype(BF16)
        r1 = da - hi.astype(F32)
        mid = r1.astype(BF16)
        lo = (r1 - mid.astype(F32)).astype(BF16)
        cs3 = _dot(tri_ref[...], jnp.concatenate([hi, mid, lo], axis=1))
        cs = cs3[:, 0:LANES] + cs3[:, LANES:2 * LANES] + cs3[:, 2 * LANES:3 * LANES]
        last = cs[UNIT - 1:UNIT, :]
        e_cs = jnp.exp(cs)
        w_end = dt * jnp.exp(last - cs)
        cs2 = cs * LOG2_E
        stacked = jnp.concatenate([w_end, e_cs], axis=0)
        s_hi, s_lo = _split2(stacked)
        unit_vals[k] = dict(cs2=cs2, cs2_t=cs2.T, dt_t=dt.T,
                            scales=jnp.concatenate([s_hi, s_lo], axis=1))

    piece_vals = {}

    def piece_slices(k, g):
        return (slice(k * UNIT, (k + 1) * UNIT), slice(g * SSD_STATE, (g + 1) * SSD_STATE),
                slice(g * SSD_GROUP_WIDTH, (g + 1) * SSD_GROUP_WIDTH))

    def piece_products(k, g):
        rows, scols, gcols = piece_slices(k, g)
        b_g = bm_ref[0, rows, scols]
        c_g = cm_ref[0, rows, scols]
        state_t = st_ref[g]
        cb = lax.dot_general(c_g, b_g, (((1,), (1,)), ((), ())), preferred_element_type=F32)
        expanded = _dot(unit_vals[k]["scales"], e2_ref[:, gcols])
        piece_vals[k, g] = dict(cb=cb, y_off=_dot(c_g, state_t.astype(BF16)),
                                w_exp=expanded[0:UNIT], e_exp=expanded[UNIT:2 * UNIT],
                                cdec_exp=expanded[2 * UNIT - 1:2 * UNIT])

    def piece_decay(k, g):
        u, p = unit_vals[k], piece_vals[k, g]
        rows, scols, gcols = piece_slices(k, g)
        x_g = xs_ref[0, rows, gcols]
        decays, x_heads = [], []
        for r in range(SSD_HEADS_PER_GROUP):
            h = g * SSD_HEADS_PER_GROUP + r
            seg2 = u["cs2"][:, h:h + 1] - u["cs2_t"][h:h + 1, :]
            decay = jnp.where(causal, jnp.exp2(seg2), 0.0) * u["dt_t"][h:h + 1, :]
            decays.append((p["cb"] * decay).astype(BF16))
            x_heads.append(jnp.where(head_lanes[r], x_g, jnp.zeros_like(x_g)))
        p["y_diag"] = _dot(jnp.concatenate(decays, axis=1), jnp.concatenate(x_heads, axis=0))
        x_dec = (x_g.astype(F32) * p["w_exp"]).astype(BF16)
        new = lax.dot_general(bm_ref[0, rows, scols], x_dec, (((0,), (0,)), ((), ())),
                              preferred_element_type=F32)
        st_ref[g] = p["cdec_exp"] * st_ref[g] + new

    def piece_output(k, g):
        u, p = unit_vals[k], piece_vals.pop((k, g))
        rows, _, gcols = piece_slices(k, g)
        y = (p["y_diag"] + p["y_off"] * p["e_exp"]
             + dexp_ref[:, gcols] * xs_ref[0, rows, gcols].astype(F32))
        yg = y * zs_ref[0, rows, gcols].astype(F32)
        ms = jnp.mean(yg * yg, axis=-1, keepdims=True)
        yn_ref[cur, rows, gcols] = (yg * lax.rsqrt(ms + RMS_EPS) * nw_ref[:, gcols]).astype(BF16)

    def merge_block(r0, c0):
        rows, cols = slice(r0, r0 + MERGE_ROWS), slice(c0, c0 + MERGE_COLS)
        y_ssd = _dot(yn_ref[prev, rows, :], wsp_ref[:, cols])
        merged_s[rows, cols] = (gyp_ref[0, rows, cols].astype(F32)
                                + gb_ref[0, rows, cols].astype(F32) * y_ssd).astype(BF16)

    def reorder_unit(k):
        rows = slice(k * UNIT, (k + 1) * UNIT)
        ordered_s[rows, :] = _dot(unperm_ref[...], merged_s[rows, :]).astype(BF16)

    def out_block(r0, c0):
        rows, cols = slice(r0, r0 + MERGE_ROWS), slice(c0, c0 + MERGE_COLS)
        resid_s[rows, cols] = (DEEPNORM_ALPHA * x_ref[0, rows, cols]
                               + _dot(ordered_s[rows, :], wout_ref[:, cols]))

    def norm_unit(k):
        rows = slice(k * UNIT, (k + 1) * UNIT)
        h1_ref[0, rows, :] = _layer_norm(resid_s[rows, :], g1_ref[...], b1_ref[...])

    col_blocks = range(0, D_MODEL, MERGE_COLS)
    finish = []
    for r0 in range(0, tile, MERGE_ROWS):
        units = range(r0 // UNIT, (r0 + MERGE_ROWS) // UNIT)
        finish += ([(merge_block, (r0, c0)) for c0 in col_blocks]
                   + [(reorder_unit, (k,)) for k in units]
                   + [(out_block, (r0, c0)) for c0 in col_blocks]
                   + [(norm_unit, (k,)) for k in units])
    pieces = [(k, g) for k in range(n_units) for g in range(SSD_GROUPS)]
    phases = (piece_products, piece_decay, piece_output)
    n_iters = len(pieces) + len(phases) - 1
    n_finish = len(finish)
    scan_prep(0)
    for n in range(n_iters):
        for lag, phase in enumerate(phases):
            if 0 <= n - lag < len(pieces):
                phase(*pieces[n - lag])
        if n % SSD_GROUPS == SSD_GROUPS // 2 and n // SSD_GROUPS + 1 < n_units:
            scan_prep(n // SSD_GROUPS + 1)
        while n_finish - len(finish) < ((n + 1) * n_finish) // n_iters:
            fin, args = finish.pop(0)
            fin(*args)


def _ssd_call(act, dt, gates, x, a_row, dexp, nw, tri, unperm, e2, wsp, wout, g1, b1):
    bsz, seq, _ = x.shape
    tile = min(SSD_TILE, seq)
    assert seq % tile == 0 and tile % UNIT == 0
    tiles_per_seq = seq // tile
    n_tiles = bsz * tiles_per_seq

    def tile_spec(width, lag):
        def index_map(t):
            i = jnp.clip(t - lag, 0, n_tiles - 1)
            return (i // tiles_per_seq, i % tiles_per_seq, 0)
        return pl.BlockSpec((1, tile, width), index_map)

    consts = (a_row, dexp, nw, tri, unperm, e2, wsp, wout, g1, b1)
    return pl.pallas_call(
        functools.partial(_ssd_kernel, tiles_per_seq),
        grid=(n_tiles + 1,),
        in_specs=[tile_spec(a.shape[-1], 0) for a in (act, dt)]
                 + [tile_spec(a.shape[-1], 1) for a in (gates, x)]
                 + [_const_spec(a.shape) for a in consts],
        out_specs=tile_spec(D_MODEL, 1),
        out_shape=jax.ShapeDtypeStruct((bsz, seq, D_MODEL), F32),
        scratch_shapes=[pltpu.VMEM((SSD_GROUPS, SSD_STATE, SSD_GROUP_WIDTH), F32),
                        pltpu.VMEM((2, tile, SSD_INNER), BF16),
                        pltpu.VMEM((tile, D_MODEL), BF16),
                        pltpu.VMEM((tile, D_MODEL), BF16),
                        pltpu.VMEM((tile, D_MODEL), F32)],
        compiler_params=pltpu.CompilerParams(
            dimension_semantics=("arbitrary",),
            vmem_limit_bytes=VMEM_LIMIT_BYTES),
        name="ssd_merge_ln1",
    )(act, dt, gates, x, *consts)


def _mlp_kernel(h_ref, wup_ref, wdn_ref, g_ref, b_ref, o_ref):
    h = h_ref[...]
    hb = h.astype(BF16)
    ff = jnp.zeros(h.shape, F32)
    for j in range(D_FF // MLP_COLS):
        cols = slice(j * MLP_COLS, (j + 1) * MLP_COLS)
        up = jnp.maximum(_dot(hb, wup_ref[:, cols]), 0.0)
        ff = ff + _dot((up * up).astype(BF16), wdn_ref[cols, :])
    o_ref[...] = _layer_norm(DEEPNORM_ALPHA * h + ff, g_ref[...], b_ref[...])


def _mlp_call(h, wup, wdn, g, b):
    rows = h.shape[0]
    tile = min(MLP_TILE, rows)
    assert rows % tile == 0
    tok = pl.BlockSpec((tile, D_MODEL), lambda i: (i, 0))
    return pl.pallas_call(
        _mlp_kernel,
        grid=(rows // tile,),
        in_specs=[tok] + [_const_spec(a.shape) for a in (wup, wdn, g, b)],
        out_specs=tok,
        out_shape=jax.ShapeDtypeStruct(h.shape, F32),
        compiler_params=pltpu.CompilerParams(
            dimension_semantics=("arbitrary",),
            vmem_limit_bytes=VMEM_LIMIT_BYTES),
        name="mlp_ln2",
    )(h, wup, wdn, g, b)


def _layer(x, w_in, b_gates, conv_w, conv_b, dt_bias, a_log, d_skip, ssd_norm_w, w_ssd_proj,
           w_pool_group, pool_scale, w_out, ln1_g, ln1_b, w_up, w_down, ln2_g, ln2_b):
    bsz, seq, _ = x.shape
    row = lambda v: v.astype(F32).reshape(1, -1)
    pad_heads = lambda v: jnp.pad(v.astype(F32), (0, LANES - SSD_HEADS)).reshape(1, LANES)

    idx = jnp.arange(UNIT)
    tok_of_row = _unit_token(idx)
    perm = (tok_of_row[:, None] == idx[None, :]).astype(BF16)
    unperm = perm.T
    tri = (tok_of_row[:, None] >= tok_of_row[None, :]).astype(BF16)
    head_of = jnp.arange(SSD_INNER) // SSD_HEAD_DIM
    e1 = (jnp.arange(LANES)[:, None] == head_of[None, :]).astype(BF16)
    e2 = jnp.concatenate([e1, e1], axis=0)

    wb = w_in.astype(BF16)
    wp, wz, wxbc = wb[:, :OFF_POOL], wb[:, OFF_POOL:OFF_Z], wb[:, OFF_Z:OFF_XBC]
    wdt = jnp.pad(wb[:, OFF_XBC:OFF_DT], ((0, 0), (0, LANES - SSD_HEADS)))
    wg = wb[:, OFF_DT:]
    act, dt, gates = _proj_call(
        x, perm, wp, wz, wxbc, wdt, wg, row(b_gates), conv_w.astype(F32), row(conv_b),
        pad_heads(dt_bias), w_pool_group.astype(BF16), row(pool_scale))

    a_row = pad_heads(-jnp.exp(a_log.astype(F32)))
    dexp = jnp.repeat(d_skip.astype(F32), SSD_HEAD_DIM).reshape(1, SSD_INNER)
    h1 = _ssd_call(act, dt, gates, x, a_row, dexp, row(ssd_norm_w), tri, unperm, e2,
                   w_ssd_proj.astype(BF16), w_out.astype(BF16), row(ln1_g), row(ln1_b))

    h2 = _mlp_call(h1.reshape(bsz * seq, D_MODEL), w_up.astype(BF16), w_down.astype(BF16),
                   row(ln2_g), row(ln2_b))
    return h2.reshape(bsz, seq, D_MODEL)


def kernel(x, w_in, b_gates, conv_w, conv_b, dt_bias, a_log, d_skip, ssd_norm_w, w_ssd_proj,
           w_pool_group, pool_scale, w_out, ln1_g, ln1_b, w_up, w_down, ln2_g, ln2_b):
    h = x
    for layer in range(w_in.shape[0]):
        h = _layer(h, w_in[layer], b_gates[layer], conv_w[layer], conv_b[layer], dt_bias[layer],
                   a_log[layer], d_skip[layer], ssd_norm_w[layer], w_ssd_proj[layer],
                   w_pool_group[layer], pool_scale[layer], w_out[layer], ln1_g[layer],
                   ln1_b[layer], w_up[layer], w_down[layer], ln2_g[layer], ln2_b[layer])
    return h
```
